```python
import jax, jax.numpy as jnp
from jax import lax
import numpy as np

D_MODEL = 2048
BATCH = 16
SEQ = 256
DEPTH = 4
DEC_BATCH = 8
DEC_SEQ = 2048
PAST_LEN = 256

GRID_W = 64
HEAD_DIM = 128
N_HEADS = D_MODEL // HEAD_DIM
H_R = N_HEADS // 2
H_A = N_HEADS - H_R
W_R = H_R * HEAD_DIM
W_A = H_A * HEAD_DIM
N_COLS = 5 * W_R + 3 * W_A
CHUNK = 64
WIN_R = 8
WIN_C = 16
ROPE_THETA = 10000.0
N_EXPERTS = 64
TOP_K = 8
N_GROUPS = 8
TOPK_GROUPS = 4
D_EXPERT = 512
D_SHARED = 512
ROUTED_SCALE = 2.5
EXPERT_BLOCK = 128
ALPHA = (2 * DEPTH) ** 0.25
BETA = (8 * DEPTH) ** -0.25
EPS = 1e-6
LB_MAX = 1.0 - 1e-4

kernel_name = "hymba_hgrn2_natten_moe_diffusion_step"


def layer_norm(x, g=None, b=None):
    xf = x.astype(jnp.float32)
    mu = xf.mean(-1, keepdims=True)
    var = jnp.square(xf - mu).mean(-1, keepdims=True)
    y = (xf - mu) * lax.rsqrt(var + EPS)
    if g is not None:
        y = y * g.astype(jnp.float32) + b.astype(jnp.float32)
    return y.astype(x.dtype)


def to_heads(t, h):
    B, L, _ = t.shape
    return t.reshape(B, L, h, -1).transpose(0, 2, 1, 3)


def merge_heads(t):
    B, H, L, d = t.shape
    return t.transpose(0, 2, 1, 3).reshape(B, L, H * d)


def rope_2d(x):
    L = x.shape[2]
    t = jnp.arange(L)
    half = HEAD_DIM // 2
    nf = half // 2
    inv = ROPE_THETA ** (-jnp.arange(nf, dtype=jnp.float32) / nf)

    def rot(xa, pos):
        ang = pos[:, None].astype(jnp.float32) * inv[None, :]
        cos = jnp.concatenate([jnp.cos(ang), jnp.cos(ang)], -1)
        sin = jnp.concatenate([jnp.sin(ang), jnp.sin(ang)], -1)
        xf = xa.astype(jnp.float32)
        x1, x2 = jnp.split(xf, 2, axis=-1)
        return xf * cos + jnp.concatenate([-x2, x1], -1) * sin

    out = jnp.concatenate([rot(x[..., :half], t // GRID_W), rot(x[..., half:], t % GRID_W)], -1)
    return out.astype(x.dtype)


def lower_bounds(lb_raw):
    p = jax.nn.softmax(lb_raw.astype(jnp.float32), axis=0)
    cs = jnp.cumsum(p, axis=0)
    lb = jnp.concatenate([jnp.zeros_like(cs[:1]), cs[:-1]], axis=0)
    return jnp.clip(lb, 0.0, LB_MAX)


def hgrn2_scan(q, k, v, log_f, S0):
    B, H, L, DK = q.shape
    DV = v.shape[-1]
    N = L // CHUNK

    def chunks(t):
        return t.reshape(B, H, N, CHUNK, t.shape[-1]).transpose(2, 0, 1, 3, 4)

    tri = jnp.tril(jnp.ones((CHUNK, CHUNK), bool))

    def step(S, inp):
        qc, kc, vc, gc = inp
        b = jnp.cumsum(gc, axis=-2)
        diff = jnp.where(tri[:, :, None], b[..., :, None, :] - b[..., None, :, :], -jnp.inf)
        A = jnp.einsum('bhtd,bhsd,bhtsd->bhts', qc, kc, jnp.exp(diff))
        o = jnp.einsum('bhts,bhsv->bhtv', A, vc) + jnp.einsum('bhtd,bhdv->bhtv', qc * jnp.exp(b), S)
        b_last = b[..., -1:, :]
        S = jnp.exp(b_last[..., 0, :])[..., None] * S + jnp.einsum('bhsd,bhsv->bhdv', kc * jnp.exp(b_last - b), vc)
        return S, o

    S, o = lax.scan(step, S0.astype(jnp.float32), (chunks(q), chunks(k), chunks(v), chunks(log_f)))
    return o.transpose(1, 2, 0, 3, 4).reshape(B, H, L, DV), S


def hgrn2_bidirectional(q, i, z_f, z_b, lb, S0_f, S0_b):
    q = q.astype(jnp.float32)
    i = i.astype(jnp.float32)

    def gates(z, lbd):
        z = z.astype(jnp.float32)
        lbd = lbd.reshape(H_R, 1, HEAD_DIM)
        log_f = jnp.logaddexp(jnp.log(lbd), jnp.log1p(-lbd) + jax.nn.log_sigmoid(z))
        k = (1.0 - lbd) * jax.nn.sigmoid(-z)
        return k, log_f

    k_f, g_f = gates(z_f, lb[0])
    k_b, g_b = gates(z_b, lb[1])
    o_f, S_f = hgrn2_scan(q, k_f, i, g_f, S0_f)
    fl = lambda t: jnp.flip(t, axis=2)
    o_b, S_b = hgrn2_scan(fl(q), fl(k_b), fl(i), fl(g_b), S0_b)
    return o_f + fl(o_b), S_f, S_b


def dense_context_attention(q, k, v):
    s = jnp.einsum('bhqd,bhkd->bhqk', q, k).astype(jnp.float32) * HEAD_DIM ** -0.5
    p = jax.nn.softmax(s, axis=-1)
    return jnp.einsum('bhqk,bhkd->bhqd', p.astype(v.dtype), v)


def neighbourhood_attention(q, k, v, k_ctx, v_ctx, rpb):
    B, H, L, DH = q.shape
    R = L // GRID_W
    KR = min(WIN_R, R)
    qg = q.reshape(B, H, R, GRID_W, DH)
    kg = k.reshape(B, H, R, GRID_W, DH)
    vg = v.reshape(B, H, R, GRID_W, DH)
    cq = jnp.arange(GRID_W)
    cs = jnp.clip(cq - WIN_C // 2, 0, GRID_W - WIN_C)
    valid = (cq[None, :] >= cs[:, None]) & (cq[None, :] < cs[:, None] + WIN_C)
    coff = jnp.clip(cq[None, :] - cq[:, None], -(WIN_C - 1), WIN_C - 1) + WIN_C - 1
    bias_c = rpb.astype(jnp.float32)[:, :, coff]
    scale = HEAD_DIM ** -0.5

    def row_block(r):
        rs = jnp.clip(r - KR // 2, 0, R - KR)
        k_blk = lax.dynamic_slice_in_dim(kg, rs, KR, axis=2)
        v_blk = lax.dynamic_slice_in_dim(vg, rs, KR, axis=2)
        q_r = lax.dynamic_index_in_dim(qg, r, axis=2, keepdims=False)
        roff = rs + jnp.arange(KR) - r + WIN_R - 1
        bias = jnp.take(bias_c, roff, axis=1).transpose(0, 2, 1, 3)
        s_loc = jnp.einsum('bhqd,bhkwd->bhqkw', q_r, k_blk).astype(jnp.float32) * scale + bias[None]
        s_loc = jnp.where(valid[:, None, :], s_loc, -jnp.inf)
        s_ctx = jnp.einsum('bhqd,bhld->bhql', q_r, k_ctx).astype(jnp.float32) * scale
        s = jnp.concatenate([s_loc.reshape(B, H, GRID_W, KR * GRID_W), s_ctx], -1)
        p = jax.nn.softmax(s, axis=-1).astype(v.dtype)
        p_loc = p[..., :KR * GRID_W].reshape(B, H, GRID_W, KR, GRID_W)
        p_ctx = p[..., KR * GRID_W:]
        return (jnp.einsum('bhqkw,bhkwd->bhqd', p_loc, v_blk)
                + jnp.einsum('bhql,bhld->bhqd', p_ctx, v_ctx))

    out = lax.map(row_block, jnp.arange(R))
    return out.transpose(1, 2, 0, 3, 4).reshape(B, H, L, DH)


def mixer(h, w_in, w_out, lb, hgrn_norm, rpb, ctx_state):
    B, L, _ = h.shape
    splits = [W_R, 2 * W_R, 3 * W_R, 4 * W_R, 5 * W_R, 5 * W_R + W_A, 5 * W_R + 2 * W_A]
    q_r, i_r, z_f, z_b, g_r, q_a, k_a, v_a = jnp.split(h @ w_in, splits, axis=-1)
    q_r = jax.nn.silu(to_heads(q_r, H_R))
    i_r, z_f, z_b = to_heads(i_r, H_R), to_heads(z_f, H_R), to_heads(z_b, H_R)
    q_a, k_a, v_a = to_heads(q_a, H_A), to_heads(k_a, H_A), to_heads(v_a, H_A)
    if ctx_state is None:
        S0 = jnp.zeros((B, 2, H_R, HEAD_DIM, HEAD_DIM), jnp.float32)
    else:
        k_ctx, v_ctx, S0 = ctx_state
    o_r, S_f, S_b = hgrn2_bidirectional(q_r, i_r, z_f, z_b, lb, S0[:, 0], S0[:, 1])
    o_r = o_r * lax.rsqrt(jnp.mean(jnp.square(o_r), -1, keepdims=True) + EPS) * hgrn_norm.astype(jnp.float32)
    o_r = merge_heads(o_r).astype(h.dtype) * jax.nn.silu(g_r)
    if ctx_state is None:
        o_a = dense_context_attention(q_a, k_a, v_a)
        new_ctx = (k_a, v_a, jnp.stack([S_f, S_b], axis=1))
    else:
        o_a = neighbourhood_attention(rope_2d(q_a), rope_2d(k_a), v_a, k_ctx, v_ctx, rpb)
        new_ctx = None
    out = jnp.concatenate([o_r, merge_heads(o_a)], -1) @ w_out
    return out, new_ctx


def swiglu(x, w_in, w_out):
    g, u = jnp.split(x @ w_in, 2, axis=-1)
    return (jax.nn.silu(g) * u) @ w_out


def routed_experts(x, eidx, w, w_e_in, w_e_out):
    T, D = x.shape
    A = T * TOP_K
    flat_e = eidx.reshape(-1)
    flat_tok = jnp.repeat(jnp.arange(T, dtype=jnp.int32), TOP_K)
    flat_w = w.reshape(-1)
    order = jnp.argsort(flat_e)
    se, st, sw = flat_e[order], flat_tok[order], flat_w[order]
    counts = jnp.bincount(flat_e, length=N_EXPERTS)
    padded = (counts + EXPERT_BLOCK - 1) // EXPERT_BLOCK * EXPERT_BLOCK
    start_orig = jnp.cumsum(counts) - counts
    end_pad = jnp.cumsum(padded)
    start_pad = end_pad - padded
    dest = start_pad[se] + jnp.arange(A) - start_orig[se]
    NB = (A + N_EXPERTS * (EXPERT_BLOCK - 1) + EXPERT_BLOCK - 1) // EXPERT_BLOCK
    P = NB * EXPERT_BLOCK
    buf_tok = jnp.full((P,), T, jnp.int32).at[dest].set(st)
    buf_w = jnp.zeros((P,), jnp.float32).at[dest].set(sw)
    blk_e = jnp.clip(jnp.searchsorted(end_pad, jnp.arange(NB) * EXPERT_BLOCK, side='right'), 0, N_EXPERTS - 1)
    x_pad = jnp.concatenate([x, jnp.zeros((1, D), x.dtype)], 0)

    def run_block(args):
        tok, e = args
        return swiglu(x_pad[tok], w_e_in[e], w_e_out[e])

    y = lax.map(run_block, (buf_tok.reshape(NB, EXPERT_BLOCK), blk_e))
    y = y.reshape(P, D).astype(jnp.float32) * buf_w[:, None]
    return jax.ops.segment_sum(y, buf_tok, num_segments=T + 1)[:T].astype(x.dtype)


def moe(h, w_router, router_bias, w_e_in, w_e_out, w_sh_in, w_sh_out):
    B, L, D = h.shape
    x = h.reshape(B * L, D)
    T = B * L
    s = jax.nn.sigmoid((x @ w_router).astype(jnp.float32))
    sb = s + router_bias.astype(jnp.float32)
    grp_score = lax.top_k(sb.reshape(T, N_GROUPS, N_EXPERTS // N_GROUPS), 2)[0].sum(-1)
    _, gidx = lax.top_k(grp_score, TOPK_GROUPS)
    gmask = jnp.any(gidx[:, :, None] == jnp.arange(N_GROUPS)[None, None, :], axis=1)
    emask = jnp.repeat(gmask, N_EXPERTS // N_GROUPS, axis=1)
    _, eidx = lax.top_k(jnp.where(emask, sb, -jnp.inf), TOP_K)
    w = jnp.take_along_axis(s, eidx, axis=1)
    w = w / w.sum(-1, keepdims=True) * ROUTED_SCALE
    out = routed_experts(x, eidx, w, w_e_in, w_e_out) + swiglu(x, w_sh_in, w_sh_out)
    return out.reshape(B, L, D)


def trunk_layer(x, cond, w_ada, b_ada, w_in, w_out, lb, hgrn_norm, rpb, ln_g, ln_b,
                w_router, router_bias, w_e_in, w_e_out, w_sh_in, w_sh_out, ctx_state):
    mod = jax.nn.silu(cond) @ w_ada + b_ada
    sh1, sc1, g1, sh2, sc2, g2 = [m[:, None, :] for m in jnp.split(mod, 6, axis=-1)]
    h = layer_norm(x) * (1 + sc1) + sh1
    mix, new_ctx = mixer(h, w_in, w_out, lb, hgrn_norm, rpb, ctx_state)
    x = layer_norm(ALPHA * x + g1 * mix, ln_g[0], ln_b[0])
    h = layer_norm(x) * (1 + sc2) + sh2
    f = moe(h, w_router, router_bias, w_e_in, w_e_out, w_sh_in, w_sh_out)
    x = layer_norm(ALPHA * x + g2 * f, ln_g[1], ln_b[1])
    return x, new_ctx


def setup_inputs(seed: int = 0) -> dict:
    key = jax.random.key(seed)
    ks = jax.random.split(key, 24)
    nrm = lambda k, shape, sc: jax.random.normal(k, shape, jnp.float32) * sc
    D = D_MODEL
    return {
        "x_prompt": nrm(ks[0], (BATCH, SEQ, D), 1.0),
        "x_sample": nrm(ks[1], (DEC_BATCH, DEC_SEQ, D), 1.0),
        "cache_k": nrm(ks[2], (DEC_BATCH, DEPTH, H_A, PAST_LEN, HEAD_DIM), 1.0),
        "cache_v": nrm(ks[3], (DEC_BATCH, DEPTH, H_A, PAST_LEN, HEAD_DIM), 1.0),
        "state_hgrn": nrm(ks[4], (DEC_BATCH, DEPTH, 2, H_R, HEAD_DIM, HEAD_DIM), 0.5),
        "c": nrm(ks[5], (DEC_BATCH, D), 1.0),
        "c_ctx": nrm(ks[6], (D,), 1.0),
        "w_ada": nrm(ks[7], (DEPTH, D, 6 * D), 0.5 * D ** -0.5),
        "b_ada": nrm(ks[8], (DEPTH, 6 * D), 0.02),
        "w_in": nrm(ks[9], (DEPTH, D, N_COLS), D ** -0.5),
        "w_out": nrm(ks[10], (DEPTH, W_R + W_A, D), BETA * (W_R + W_A) ** -0.5),
        "lb_raw": nrm(ks[11], (DEPTH, 2, W_R), 0.5),
        "hgrn_norm": 1.0 + nrm(ks[12], (DEPTH, HEAD_DIM), 0.02),
        "rpb": nrm(ks[13], (DEPTH, H_A, 2 * WIN_R - 1, 2 * WIN_C - 1), 0.1),
        "ln_g": 1.0 + nrm(ks[14], (DEPTH, 2, D), 0.02),
        "ln_b": nrm(ks[15], (DEPTH, 2, D), 0.02),
        "w_router": nrm(ks[16], (DEPTH, D, N_EXPERTS), D ** -0.5),
        "router_bias": nrm(ks[17], (DEPTH, N_EXPERTS), 0.01),
        "w_e_in": nrm(ks[18], (DEPTH, N_EXPERTS, D, 2 * D_EXPERT), D ** -0.5),
        "w_e_out": nrm(ks[19], (DEPTH, N_EXPERTS, D_EXPERT, D), BETA * D_EXPERT ** -0.5),
        "w_sh_in": nrm(ks[20], (DEPTH, D, 2 * D_SHARED), D ** -0.5),
        "w_sh_out": nrm(ks[21], (DEPTH, D_SHARED, D), BETA * D_SHARED ** -0.5),
    }


def reference(x_prompt, x_sample, cache_k, cache_v, state_hgrn, c, c_ctx,
              w_ada, b_ada, w_in, w_out, lb_raw, hgrn_norm, rpb, ln_g, ln_b,
              w_router, router_bias, w_e_in, w_e_out, w_sh_in, w_sh_out):
    lb_all = lower_bounds(lb_raw)
    cond_ctx = c_ctx[None, :]
    xp, xs = x_prompt, x_sample
    new_k, new_v, new_s = [], [], []
    for l in range(DEPTH):
        params = (w_ada[l], b_ada[l], w_in[l], w_out[l], lb_all[l], hgrn_norm[l], rpb[l],
                  ln_g[l], ln_b[l], w_router[l], router_bias[l], w_e_in[l], w_e_out[l],
                  w_sh_in[l], w_sh_out[l])
        xp, (k_l, v_l, s_l) = trunk_layer(xp, cond_ctx, *params, ctx_state=None)
        xs, _ = trunk_layer(xs, c, *params,
                            ctx_state=(cache_k[:, l], cache_v[:, l], state_hgrn[:, l]))
        new_k.append(k_l)
        new_v.append(v_l)
        new_s.append(s_l)
    new_cache_k = jnp.stack(new_k, axis=1)
    new_cache_v = jnp.stack(new_v, axis=1)
    new_state_hgrn = jnp.stack(new_s, axis=1)
    return (xp, xs, new_cache_k, new_cache_v, new_state_hgrn)
```

```python
import functools

import jax
import jax.numpy as jnp
from jax import lax
from jax.experimental import pallas as pl
from jax.experimental.pallas import tpu as pltpu

F32 = jnp.float32
BF16 = jnp.bfloat16

D_MODEL = 2048
HEAD_DIM = 128
N_HEADS = D_MODEL // HEAD_DIM
H_R = N_HEADS // 2
H_A = N_HEADS - H_R
W_R = H_R * HEAD_DIM
W_A = H_A * HEAD_DIM
N_COLS = 5 * W_R + 3 * W_A
N_COLBLK = N_COLS // HEAD_DIM
CHUNK = 64
SUB = 16
GRID_W = 64
WIN_R = 8
WIN_C = 16
ROPE_THETA = 10000.0
N_EXPERTS = 64
TOP_K = 8
N_GROUPS = 8
GROUP_SIZE = N_EXPERTS // N_GROUPS
TOPK_GROUPS = 4
D_EXPERT = 512
D_SHARED = 512
ROUTED_SCALE = 2.5
EPS = 1e-6
LB_MAX = 1.0 - 1e-4
NEG_BIG = -1e30
EXP_CLAMP = 80.0

MOE_TILE = 256
ROW_BLK = 16
TILE_ROWS = MOE_TILE * TOP_K + N_EXPERTS * ROW_BLK
TILE_BLKS = TILE_ROWS // ROW_BLK
W_LANES = 128
XS_COLS = D_MODEL + W_LANES
DISP_ROWS = 512
COMB_ROWS = 768
EXP_GROUP = 16
VMEM_LIMIT = 52 * 1024 * 1024


def _cparams(sem):
    return pltpu.CompilerParams(dimension_semantics=sem, vmem_limit_bytes=VMEM_LIMIT)


def _dot(a, b):
    return jnp.dot(a, b, preferred_element_type=F32)


def _dot_nt(a, b):
    return lax.dot_general(a, b, (((1,), (1,)), ((), ())), preferred_element_type=F32)


def _dot_tn(a, b):
    return lax.dot_general(a, b, (((0,), (0,)), ((), ())), preferred_element_type=F32)


def _silu(x):
    return x * jax.nn.sigmoid(x)


def _layer_norm(x):
    mu = jnp.mean(x, axis=-1, keepdims=True)
    xc = x - mu
    var = jnp.mean(xc * xc, axis=-1, keepdims=True)
    return xc * lax.rsqrt(var + EPS)


def _split3(x):
    a = x.astype(BF16)
    r = x - a.astype(F32)
    b = r.astype(BF16)
    c = (r - b.astype(F32)).astype(BF16)
    return a, b, c


def _mod_kernel(c_ref, w_ref, b_ref, o_ref):
    a = _silu(c_ref[...]).astype(BF16)
    o_ref[...] = _dot(a, w_ref[...].astype(BF16)) + b_ref[...]


def _modulation(cond, w_ada, b_ada):
    depth, d, n6 = w_ada.shape
    nc = cond.shape[0]
    tn = 1024
    return pl.pallas_call(
        _mod_kernel,
        grid=(depth, n6 // tn),
        in_specs=[
            pl.BlockSpec((nc, d), lambda l, j: (0, 0)),
            pl.BlockSpec((None, d, tn), lambda l, j: (l, 0, j)),
            pl.BlockSpec((None, 1, tn), lambda l, j: (l, 0, j)),
        ],
        out_specs=pl.BlockSpec((None, nc, tn), lambda l, j: (l, 0, j)),
        out_shape=jax.ShapeDtypeStruct((depth, nc, n6), F32),
        compiler_params=_cparams(("arbitrary", "arbitrary")),
        name="modulation",
    )(cond, w_ada, b_ada.reshape(depth, 1, n6))


def _inproj_kernel(x_ref, mod_ref, w_ref, o_ref, h_scr):
    @pl.when(pl.program_id(1) == 0)
    def _():
        y = _layer_norm(x_ref[...])
        sh = mod_ref[:, 0:D_MODEL]
        sc = mod_ref[:, D_MODEL:2 * D_MODEL]
        h_scr[...] = (y * (1.0 + sc) + sh).astype(BF16)

    res = _dot(h_scr[...], w_ref[...]).astype(BF16)
    for j in range(o_ref.shape[0]):
        o_ref[j] = res[:, j * HEAD_DIM:(j + 1) * HEAD_DIM]


def _inproj(x, mod_l, w_in_l, tm, cond_of_tile):
    t = x.shape[0]
    tn = 1024
    nblk = tn // HEAD_DIM
    nc = mod_l.shape[0]
    return pl.pallas_call(
        _inproj_kernel,
        grid=(t // tm, N_COLS // tn),
        in_specs=[
            pl.BlockSpec((tm, D_MODEL), lambda i, j: (i, 0)),
            pl.BlockSpec((None, 1, 6 * D_MODEL), lambda i, j: (cond_of_tile(i), 0, 0)),
            pl.BlockSpec((D_MODEL, tn), lambda i, j: (0, j)),
        ],
        out_specs=pl.BlockSpec((nblk, tm, HEAD_DIM), lambda i, j: (j, i, 0)),
        out_shape=jax.ShapeDtypeStruct((N_COLBLK, t, HEAD_DIM), BF16),
        scratch_shapes=[pltpu.VMEM((tm, D_MODEL), BF16)],
        compiler_params=_cparams(("arbitrary", "arbitrary")),
        name="inproj",
    )(x, mod_l.reshape(nc, 1, 6 * D_MODEL), w_in_l)


def _gates(z, log_lb, log1m_lb, one_m_lb):
    lse = jnp.log(1.0 + jnp.exp(-jnp.abs(z)))
    log_sig = jnp.minimum(z, 0.0) - lse
    k = one_m_lb * jnp.exp(log_sig - z)
    c = log1m_lb + log_sig
    g = jnp.maximum(log_lb, c) + jnp.log(1.0 + jnp.exp(-jnp.abs(log_lb - c)))
    return k, g


def _chunk_cumsum(tri, g):
    g1, g2, g3 = _split3(g)
    return _dot(tri, g1) + _dot(tri, g2) + _dot(tri, g3)


def _hgrn_chunk(q, v, z, st, lbc, tri, reverse):
    k, g = _gates(z, lbc[0], lbc[1], lbc[2])
    b = _chunk_cumsum(tri, g)
    vb = v.astype(BF16)
    outs = []
    for blk in range(CHUNK // SUB):
        lo = blk * SUB
        if reverse:
            mid, ks, ke = lo + SUB // 2, lo, CHUNK
        else:
            mid, ks, ke = lo + SUB // 2 - 1, 0, lo + SUB
        n = ke - ks
        r = b[mid:mid + 1, :]
        qe = q[lo:lo + SUB] * jnp.exp(jnp.minimum(b[lo:lo + SUB] - r, EXP_CLAMP))
        kx = k[ks:ke] * jnp.exp(jnp.minimum(r - b[ks:ke], EXP_CLAMP))
        a = _dot_nt(qe.astype(BF16), kx.astype(BF16))
        row = lax.broadcasted_iota(jnp.int32, (SUB, n), 0)
        col = lax.broadcasted_iota(jnp.int32, (SUB, n), 1)
        keep = (col >= row) if reverse else (col <= row + lo)
        a = jnp.where(keep, a, 0.0)
        outs.append(_dot(a.astype(BF16), vb[ks:ke]))
    o_intra = jnp.concatenate(outs, axis=0)
    o_inter = _dot_nt((q * jnp.exp(b)).astype(BF16), st.astype(BF16))
    edge = b[0:1, :] if reverse else b[CHUNK - 1:CHUNK, :]
    kl = k * jnp.exp(edge - b)
    st_new = st * jnp.exp(edge) + _dot_tn(vb, kl.astype(BF16))
    return o_intra + o_inter, st_new


def _hgrn_kernel(*refs, seq_len, has_init, emit_state):
    q_ref, i_ref, zf_ref, zb_ref, g_ref, lbc_ref, nrm_ref = refs[:7]
    pos = 7
    s0_ref = None
    if has_init:
        s0_ref = refs[pos]
        pos += 1
    o_ref = refs[pos]
    pos += 1
    so_ref = None
    if emit_state:
        so_ref = refs[pos]
        pos += 1
    of_scr, ob_scr = refs[pos], refs[pos + 1]

    n_chunks = seq_len // CHUNK
    ri = lax.broadcasted_iota(jnp.int32, (CHUNK, CHUNK), 0)
    ci = lax.broadcasted_iota(jnp.int32, (CHUNK, CHUNK), 1)
    tri_f = jnp.where(ci <= ri, 1.0, 0.0).astype(BF16)
    tri_b = jnp.where(ci >= ri, 1.0, 0.0).astype(BF16)
    lbc = lbc_ref[...]
    lbc_f = (lbc[0:1], lbc[1:2], lbc[2:3])
    lbc_b = (lbc[3:4], lbc[4:5], lbc[5:6])

    if has_init:
        st_f0 = s0_ref[0].T
        st_b0 = s0_ref[1].T
    else:
        st_f0 = jnp.zeros((HEAD_DIM, HEAD_DIM), F32)
        st_b0 = jnp.zeros((HEAD_DIM, HEAD_DIM), F32)

    def body(c, carry):
        st_f, st_b = carry
        rf = pl.multiple_of(c * CHUNK, CHUNK)
        rb = pl.multiple_of((n_chunks - 1 - c) * CHUNK, CHUNK)
        qf = _silu(q_ref[pl.ds(rf, CHUNK), :].astype(F32))
        o_f, st_f = _hgrn_chunk(qf, i_ref[pl.ds(rf, CHUNK), :], zf_ref[pl.ds(rf, CHUNK), :].astype(F32),
                                st_f, lbc_f, tri_f, False)
        of_scr[pl.ds(rf, CHUNK), :] = o_f
        qb = _silu(q_ref[pl.ds(rb, CHUNK), :].astype(F32))
        o_b, st_b = _hgrn_chunk(qb, i_ref[pl.ds(rb, CHUNK), :], zb_ref[pl.ds(rb, CHUNK), :].astype(F32),
                                st_b, lbc_b, tri_b, True)
        ob_scr[pl.ds(rb, CHUNK), :] = o_b
        return st_f, st_b

    st_f, st_b = lax.fori_loop(0, n_chunks, body, (st_f0, st_b0))
    if emit_state:
        so_ref[0] = st_f.T
        so_ref[1] = st_b.T

    nrm = nrm_ref[...]
    piece = 256 if seq_len % 256 == 0 else CHUNK

    def fin(p, carry):
        r0 = pl.multiple_of(p * piece, piece)
        o = of_scr[pl.ds(r0, piece), :] + ob_scr[pl.ds(r0, piece), :]
        o = o * lax.rsqrt(jnp.mean(o * o, axis=-1, keepdims=True) + EPS) * nrm
        gate = g_ref[pl.ds(r0, piece), :].astype(F32)
        o_ref[pl.ds(r0, piece), :] = (o * _silu(gate)).astype(BF16)
        return carry

    lax.fori_loop(0, seq_len // piece, fin, 0)


def _hgrn(proj, lbc, nrm, seq_len, n_seq, row_blk0, state_in=None, layer=0, emit_state=False):
    kern = functools.partial(_hgrn_kernel, seq_len=seq_len, has_init=state_in is not None, emit_state=emit_state)

    def col(sec):
        return pl.BlockSpec((None, seq_len, HEAD_DIM), lambda b, h: (sec * H_R + h, row_blk0 + b, 0))

    in_specs = [col(0), col(1), col(2), col(3), col(4),
                pl.BlockSpec((8, HEAD_DIM), lambda b, h: (0, h)),
                pl.BlockSpec((1, HEAD_DIM), lambda b, h: (0, 0))]
    args = [proj, proj, proj, proj, proj, lbc, nrm]
    if state_in is not None:
        in_specs.append(pl.BlockSpec((None, None, 2, None, HEAD_DIM, HEAD_DIM),
                                     lambda b, h: (b, layer, 0, h, 0, 0)))
        args.append(state_in)
    out_specs = [pl.BlockSpec((seq_len, HEAD_DIM), lambda b, h: (b, h))]
    out_shape = [jax.ShapeDtypeStruct((n_seq * seq_len, W_R), BF16)]
    if emit_state:
        out_specs.append(pl.BlockSpec((None, 2, None, HEAD_DIM, HEAD_DIM), lambda b, h: (b, 0, h, 0, 0)))
        out_shape.append(jax.ShapeDtypeStruct((n_seq, 2, H_R, HEAD_DIM, HEAD_DIM), F32))
    res = pl.pallas_call(
        kern,
        grid=(n_seq, H_R),
        in_specs=in_specs,
        out_specs=out_specs,
        out_shape=out_shape,
        scratch_shapes=[pltpu.VMEM((seq_len, HEAD_DIM), F32), pltpu.VMEM((seq_len, HEAD_DIM), F32)],
        compiler_params=_cparams(("arbitrary", "arbitrary")),
        name="hgrn_state" if emit_state else "hgrn",
    )(*args)
    return res


def _ctx_attn_kernel(q_ref, k_ref, v_ref, o_ref, ko_ref, vo_ref):
    q = q_ref[...]
    k = k_ref[...]
    v = v_ref[...]
    s = _dot_nt(q, k) * (HEAD_DIM ** -0.5)
    m = jnp.max(s, axis=-1, keepdims=True)
    p = jnp.exp(s - m)
    den = jnp.sum(p, axis=-1, keepdims=True)
    o = _dot(p.astype(BF16), v) / den
    o_ref[...] = o.astype(BF16)
    ko_ref[...] = k.astype(F32)
    vo_ref[...] = v.astype(F32)


def _ctx_attention(proj, seq_len, n_seq):
    def col(sec):
        return pl.BlockSpec((None, seq_len, HEAD_DIM), lambda b, h: (5 * H_R + sec * H_A + h, b, 0))

    cache_spec = pl.BlockSpec((None, None, seq_len, HEAD_DIM), lambda b, h: (b, h, 0, 0))
    cache_shape = jax.ShapeDtypeStruct((n_seq, H_A, seq_len, HEAD_DIM), F32)
    return pl.pallas_call(
        _ctx_attn_kernel,
        grid=(n_seq, H_A),
        in_specs=[col(0), col(1), col(2)],
        out_specs=[pl.BlockSpec((seq_len, HEAD_DIM), lambda b, h: (b, h)), cache_spec, cache_spec],
        out_shape=[jax.ShapeDtypeStruct((n_seq * seq_len, W_A), BF16), cache_shape, cache_shape],
        compiler_params=_cparams(("arbitrary", "arbitrary")),
        name="ctx_attention",
    )(proj, proj, proj)


def _rope(x, cos, sin_lo, sin_hi):
    return (x * cos + pltpu.roll(x, HEAD_DIM - HEAD_DIM // 4, axis=1) * sin_lo
            + pltpu.roll(x, HEAD_DIM // 4, axis=1) * sin_hi)


def _nbr_attn_kernel(q_ref, k_ref, v_ref, kc_ref, vc_ref, bias_ref, cos_ref, slo_ref, shi_ref, o_ref, kr_scr,
                     *, n_rows):
    rows_pc = 256

    def rope_k(p, carry):
        r0 = pl.multiple_of(p * rows_pc, rows_pc)
        sl = pl.ds(r0, rows_pc)
        kr_scr[sl, :] = _rope(k_ref[sl, :].astype(F32), cos_ref[sl, :], slo_ref[sl, :], shi_ref[sl, :]).astype(BF16)
        return carry

    lax.fori_loop(0, (n_rows * GRID_W) // rows_pc, rope_k, 0)
    kc = kc_ref[...].astype(BF16)
    vc = vc_ref[...].astype(BF16)
    scale = HEAD_DIM ** -0.5
    win = WIN_R * GRID_W

    def row(r, carry):
        rs = jnp.clip(r - WIN_R // 2, 0, n_rows - WIN_R)
        dl = rs - r + WIN_R - 1
        q0 = pl.multiple_of(r * GRID_W, GRID_W)
        k0 = pl.multiple_of(rs * GRID_W, GRID_W)
        qs = pl.ds(q0, GRID_W)
        q = _rope(q_ref[qs, :].astype(F32), cos_ref[qs, :], slo_ref[qs, :], shi_ref[qs, :])
        q = (q * scale).astype(BF16)
        s_loc = _dot_nt(q, kr_scr[pl.ds(k0, win), :]) + bias_ref[dl]
        s_ctx = _dot_nt(q, kc)
        m = jnp.maximum(jnp.max(s_loc, axis=-1, keepdims=True), jnp.max(s_ctx, axis=-1, keepdims=True))
        p_loc = jnp.exp(s_loc - m)
        p_ctx = jnp.exp(s_ctx - m)
        den = jnp.sum(p_loc, axis=-1, keepdims=True) + jnp.sum(p_ctx, axis=-1, keepdims=True)
        o = _dot(p_loc.astype(BF16), v_ref[pl.ds(k0, win), :]) + _dot(p_ctx.astype(BF16), vc)
        o_ref[qs, :] = (o / den).astype(BF16)
        return carry

    lax.fori_loop(0, n_rows, row, 0)


def _nbr_attention(proj, cache_k, cache_v, bias_tab, rope_tabs, layer, seq_len, n_seq, row_blk0):
    n_rows = seq_len // GRID_W
    past = cache_k.shape[3]
    kern = functools.partial(_nbr_attn_kernel, n_rows=n_rows)

    def col(sec):
        return pl.BlockSpec((None, seq_len, HEAD_DIM), lambda b, h: (5 * H_R + sec * H_A + h, row_blk0 + b, 0))

    cache_spec = pl.BlockSpec((None, None, None, past, HEAD_DIM), lambda b, h: (b, layer, h, 0, 0))
    tab_spec = pl.BlockSpec((seq_len, HEAD_DIM), lambda b, h: (0, 0))
    return pl.pallas_call(
        kern,
        grid=(n_seq, H_A),
        in_specs=[col(0), col(1), col(2), cache_spec, cache_spec,
                  pl.BlockSpec((None, WIN_R, GRID_W, WIN_R * GRID_W), lambda b, h: (h, 0, 0, 0)),
                  tab_spec, tab_spec, tab_spec],
        out_specs=pl.BlockSpec((seq_len, HEAD_DIM), lambda b, h: (b, h)),
        out_shape=jax.ShapeDtypeStruct((n_seq * seq_len, W_A), BF16),
        scratch_shapes=[pltpu.VMEM((seq_len, HEAD_DIM), BF16)],
        compiler_params=_cparams(("arbitrary", "arbitrary")),
        name="nbr_attention",
    )(proj, proj, proj, cache_k, cache_v, bias_tab, *rope_tabs)


def _first_argmax_rows(vals, ids, sentinel):
    m = vals[0]
    for v in vals[1:]:
        m = jnp.maximum(m, v)
    m = jnp.max(m, axis=0, keepdims=True)
    best = None
    for v, i in zip(vals, ids):
        cand = jnp.min(jnp.where(v == m, i, sentinel), axis=0, keepdims=True)
        best = cand if best is None else jnp.minimum(best, cand)
    return m, best


def _route(s, sb):
    n = s.shape[1]
    iota8 = lax.broadcasted_iota(jnp.int32, (GROUP_SIZE, n), 0)
    neg_inf = -jnp.inf
    groups = [sb[g * GROUP_SIZE:(g + 1) * GROUP_SIZE, :] for g in range(N_GROUPS)]
    gscore = jnp.zeros((N_GROUPS, n), F32)
    for g, xg in enumerate(groups):
        m1, i1 = _first_argmax_rows([xg], [iota8], GROUP_SIZE)
        m2 = jnp.max(jnp.where(iota8 == i1, neg_inf, xg), axis=0, keepdims=True)
        gscore = jnp.where(iota8 == g, m1 + m2, gscore)
    gsel = jnp.zeros((N_GROUPS, n), F32)
    cur = gscore
    for _ in range(TOPK_GROUPS):
        _, i = _first_argmax_rows([cur], [iota8], N_GROUPS)
        hit = iota8 == i
        gsel = jnp.where(hit, 1.0, gsel)
        cur = jnp.where(hit, neg_inf, cur)
    cur = [jnp.where(gsel[g:g + 1, :] > 0.5, xg, neg_inf) for g, xg in enumerate(groups)]
    ids = [iota8 + g * GROUP_SIZE for g in range(N_GROUPS)]
    sel = [jnp.zeros((GROUP_SIZE, n), jnp.bool_) for _ in range(N_GROUPS)]
    for _ in range(TOP_K):
        _, i = _first_argmax_rows(cur, ids, N_EXPERTS)
        hits = [idg == i for idg in ids]
        sel = [jnp.logical_or(a, h) for a, h in zip(sel, hits)]
        cur = [jnp.where(h, neg_inf, c) for c, h in zip(cur, hits)]
    return sel


def _outproj_kernel(or_ref, oa_ref, x_ref, mod_ref, w_ref, lng_ref, lnb_ref, wr_ref, rb_ref,
                    x1_ref, h2_ref, wt_ref, sel_ref, nb_ref, *, alpha):
    mix = _dot(or_ref[...], w_ref[0:W_R, :]) + _dot(oa_ref[...], w_ref[W_R:W_R + W_A, :])
    g1 = mod_ref[:, 2 * D_MODEL:3 * D_MODEL]
    x1 = _layer_norm(alpha * x_ref[...] + g1 * mix) * lng_ref[0:1, :] + lnb_ref[0:1, :]
    x1_ref[...] = x1
    sh2 = mod_ref[:, 3 * D_MODEL:4 * D_MODEL]
    sc2 = mod_ref[:, 4 * D_MODEL:5 * D_MODEL]
    h2 = (_layer_norm(x1) * (1.0 + sc2) + sh2).astype(BF16)
    h2_ref[...] = h2
    s = jax.nn.sigmoid(_dot_nt(wr_ref[...], h2))
    sel = _route(s, s + rb_ref[...])
    n = s.shape[1]
    wsel = [jnp.where(sel[g], s[g * GROUP_SIZE:(g + 1) * GROUP_SIZE, :], 0.0) for g in range(N_GROUPS)]
    tot = wsel[0]
    for w in wsel[1:]:
        tot = tot + w
    tot = jnp.sum(tot, axis=0, keepdims=True)
    for g in range(N_GROUPS):
        sl = slice(g * GROUP_SIZE, (g + 1) * GROUP_SIZE)
        wt_ref[sl, :] = wsel[g] / tot * ROUTED_SCALE
        sel_ref[sl, :] = jnp.where(sel[g], 1.0, 0.0)
    selv = sel_ref[...].astype(BF16)
    cnt = _dot_nt(jnp.ones((8, n), BF16), selv)
    nb_ref[...] = jnp.floor((cnt + (ROW_BLK - 1)) * (1.0 / ROW_BLK))


def _outproj(o_r, o_a, x, mod_l, w_out_l, ln_g_l, ln_b_l, wr_t, rbias, alpha, cond_of_tile):
    t = x.shape[0]
    tm = MOE_TILE
    nt = t // tm
    nc = mod_l.shape[0]
    kern = functools.partial(_outproj_kernel, alpha=alpha)
    row = lambda i: (i, 0)
    const = lambda i: (0, 0)
    return pl.pallas_call(
        kern,
        grid=(nt,),
        in_specs=[
            pl.BlockSpec((tm, W_R), row),
            pl.BlockSpec((tm, W_A), row),
            pl.BlockSpec((tm, D_MODEL), row),
            pl.BlockSpec((None, 1, 6 * D_MODEL), lambda i: (cond_of_tile(i), 0, 0)),
            pl.BlockSpec((W_R + W_A, D_MODEL), const),
            pl.BlockSpec((2, D_MODEL), const),
            pl.BlockSpec((2, D_MODEL), const),
            pl.BlockSpec((N_EXPERTS, D_MODEL), const),
            pl.BlockSpec((N_EXPERTS, 1), const),
        ],
        out_specs=[
            pl.BlockSpec((tm, D_MODEL), row),
            pl.BlockSpec((tm, D_MODEL), row),
            pl.BlockSpec((N_EXPERTS, tm), lambda i: (0, i)),
            pl.BlockSpec((N_EXPERTS, tm), lambda i: (0, i)),
            pl.BlockSpec((None, 8, N_EXPERTS), lambda i: (i, 0, 0)),
        ],
        out_shape=[
            jax.ShapeDtypeStruct((t, D_MODEL), F32),
            jax.ShapeDtypeStruct((t, D_MODEL), BF16),
            jax.ShapeDtypeStruct((N_EXPERTS, t), F32),
            jax.ShapeDtypeStruct((N_EXPERTS, t), F32),
            jax.ShapeDtypeStruct((nt, 8, N_EXPERTS), F32),
        ],
        compiler_params=_cparams(("arbitrary",)),
        name="outproj_router",
    )(o_r, o_a, x, mod_l.reshape(nc, 1, 6 * D_MODEL), w_out_l, ln_g_l, ln_b_l, wr_t, rbias)


def _sorted_layout(sel):
    n = sel.shape[1]
    selb = sel.astype(BF16)
    ti = lax.broadcasted_iota(jnp.int32, (n, n), 0)
    tj = lax.broadcasted_iota(jnp.int32, (n, n), 1)
    rank = _dot(selb, jnp.where(ti < tj, 1.0, 0.0).astype(BF16))
    cnt = _dot(selb, jnp.ones((n, n), BF16))
    nb = jnp.floor((cnt + (ROW_BLK - 1)) * (1.0 / ROW_BLK))
    ei = lax.broadcasted_iota(jnp.int32, (N_EXPERTS, N_EXPERTS), 0)
    ej = lax.broadcasted_iota(jnp.int32, (N_EXPERTS, N_EXPERTS), 1)
    boff = _dot(jnp.where(ej < ei, 1.0, 0.0).astype(BF16), nb.astype(BF16))
    eye = lax.broadcasted_iota(jnp.int32, (N_EXPERTS, n), 0) == lax.broadcasted_iota(jnp.int32, (N_EXPERTS, n), 1)
    ones8 = jnp.ones((8, N_EXPERTS), BF16)
    boff_row = _dot(ones8, jnp.where(eye, boff, 0.0).astype(BF16))[0:1, 0:N_EXPERTS]
    nb_row = _dot(ones8, jnp.where(eye, nb, 0.0).astype(BF16))[0:1, 0:N_EXPERTS]
    rankp = jnp.where(sel > 0.5, rank + 1.0, 0.0)
    return rankp, boff, boff_row, nb_row


def _row_expert_onehot(row0, n_rows, boff_row, nb_row):
    blk = (lax.broadcasted_iota(jnp.int32, (n_rows, N_EXPERTS), 0) + row0) // ROW_BLK
    blk = blk.astype(F32)
    return jnp.where(jnp.logical_and(blk >= boff_row, blk < boff_row + nb_row), 1.0, 0.0).astype(BF16)


def _dispatch_kernel(used_ref, h2_ref, sel_ref, wt_ref, xs_ref, rank_scr, boff_scr, row_scr, w_scr):
    s = pl.program_id(0)
    rc = pl.program_id(1)

    @pl.when(rc == 0)
    def _():
        rankp, boff, boff_row, nb_row = _sorted_layout(sel_ref[...])
        rank_scr[...] = rankp.astype(BF16)
        boff_scr[...] = boff.astype(BF16)
        row_scr[0:1, :] = boff_row
        row_scr[1:2, :] = nb_row
        w1, w2, w3 = _split3(wt_ref[...])
        w_scr[0] = w1
        w_scr[1] = w2
        w_scr[2] = w3

    row0 = rc * DISP_ROWS

    @pl.when(row0 < used_ref[s])
    def _():
        n = h2_ref.shape[0]
        onehot_e = _row_expert_onehot(row0, DISP_ROWS, row_scr[0:1, :], row_scr[1:2, :])
        q1 = _dot(onehot_e, rank_scr[...])
        kb = _dot(onehot_e, boff_scr[...])
        krow = (lax.broadcasted_iota(jnp.int32, (DISP_ROWS, n), 0) + (row0 + 1)).astype(F32) - ROW_BLK * kb
        hit = q1 == krow
        xs_ref[:, 0:D_MODEL] = _dot(jnp.where(hit, 1.0, 0.0).astype(BF16), h2_ref[...]).astype(BF16)
        wq = _dot(onehot_e, w_scr[0]) + _dot(onehot_e, w_scr[1]) + _dot(onehot_e, w_scr[2])
        ws = jnp.sum(jnp.where(hit, wq, 0.0), axis=1, keepdims=True)
        p1, p2, p3 = (p.astype(F32) for p in _split3(ws))
        lane = lax.broadcasted_iota(jnp.int32, (DISP_ROWS, W_LANES), 1)
        wl = jnp.where(lane == 0, p1, jnp.where(lane == 1, p2, jnp.where(lane == 2, p3, 0.0)))
        xs_ref[:, D_MODEL:XS_COLS] = wl.astype(BF16)

    @pl.when(row0 >= used_ref[s])
    def _():
        xs_ref[...] = jnp.zeros(xs_ref.shape, BF16)


def _dispatch(used_rows, h2, sel_t, w_t):
    t = h2.shape[0]
    nt = t // MOE_TILE
    return pl.pallas_call(
        _dispatch_kernel,
        grid_spec=pltpu.PrefetchScalarGridSpec(
            num_scalar_prefetch=1,
            grid=(nt, TILE_ROWS // DISP_ROWS),
            in_specs=[
                pl.BlockSpec((MOE_TILE, D_MODEL), lambda s, r, u: (s, 0)),
                pl.BlockSpec((N_EXPERTS, MOE_TILE), lambda s, r, u: (0, s)),
                pl.BlockSpec((N_EXPERTS, MOE_TILE), lambda s, r, u: (0, s)),
            ],
            out_specs=pl.BlockSpec((DISP_ROWS, XS_COLS), lambda s, r, u: (s * (TILE_ROWS // DISP_ROWS) + r, 0)),
            scratch_shapes=[
                pltpu.VMEM((N_EXPERTS, MOE_TILE), BF16),
                pltpu.VMEM((N_EXPERTS, MOE_TILE), BF16),
                pltpu.VMEM((8, N_EXPERTS), F32),
                pltpu.VMEM((3, N_EXPERTS, MOE_TILE), BF16),
            ],
        ),
        out_shape=jax.ShapeDtypeStruct((nt * TILE_ROWS, XS_COLS), BF16),
        compiler_params=_cparams(("arbitrary", "arbitrary")),
        name="moe_dispatch",
    )(used_rows, h2, sel_t, w_t)


def _expert_kernel(bstart_ref, bsrc_ref, xs_hbm, w1_ref, w2_ref, ys_hbm, w1_scr, w2_scr, x_buf, y_buf, sem_in, sem_out):
    e = pl.program_id(0)
    w1_scr[...] = w1_ref[...].astype(BF16)
    w2_scr[...] = w2_ref[...].astype(BF16)

    @pl.when(e == 0)
    def _():
        x_buf[...] = jnp.zeros(x_buf.shape, BF16)

    b0 = bstart_ref[e]
    n_blk = bstart_ref[e + 1] - b0
    n_groups = (n_blk + EXP_GROUP - 1) // EXP_GROUP

    def in_copy(j, k):
        src = pl.multiple_of(bsrc_ref[j] * ROW_BLK, ROW_BLK)
        return pltpu.make_async_copy(xs_hbm.at[pl.ds(src, ROW_BLK), :],
                                     x_buf.at[pl.ds(k * ROW_BLK, ROW_BLK), :], sem_in)

    def out_copy(j, k):
        dst = pl.multiple_of(bsrc_ref[j] * ROW_BLK, ROW_BLK)
        return pltpu.make_async_copy(y_buf.at[pl.ds(k * ROW_BLK, ROW_BLK), :],
                                     ys_hbm.at[pl.ds(dst, ROW_BLK), pl.ds(0, D_MODEL)], sem_out)

    def group(gi, carry):
        k0 = gi * EXP_GROUP
        for k in range(EXP_GROUP):
            @pl.when(k0 + k < n_blk)
            def _():
                in_copy(b0 + k0 + k, k).start()
        for k in range(EXP_GROUP):
            @pl.when(k0 + k < n_blk)
            def _():
                in_copy(b0 + k0 + k, k).wait()
        x = x_buf[:, 0:D_MODEL]
        wrow = jnp.sum(x_buf[:, D_MODEL:XS_COLS].astype(F32), axis=1, keepdims=True)
        h = _dot(x, w1_scr[...])
        act = (_silu(h[:, 0:D_EXPERT]) * h[:, D_EXPERT:2 * D_EXPERT]).astype(BF16)
        y_buf[...] = (_dot(act, w2_scr[...]) * wrow).astype(BF16)
        for k in range(EXP_GROUP):
            @pl.when(k0 + k < n_blk)
            def _():
                out_copy(b0 + k0 + k, k).start()
        for k in range(EXP_GROUP):
            @pl.when(k0 + k < n_blk)
            def _():
                out_copy(b0 + k0 + k, k).wait()
        return carry

    lax.fori_loop(0, n_groups, group, 0)


def _experts(bstart, bsrc, xs, w_e_in, w_e_out, layer):
    rows = EXP_GROUP * ROW_BLK
    return pl.pallas_call(
        _expert_kernel,
        grid_spec=pltpu.PrefetchScalarGridSpec(
            num_scalar_prefetch=2,
            grid=(N_EXPERTS,),
            in_specs=[
                pl.BlockSpec(memory_space=pl.ANY),
                pl.BlockSpec((None, None, D_MODEL, 2 * D_EXPERT), lambda e, bs, br: (layer, e, 0, 0)),
                pl.BlockSpec((None, None, D_EXPERT, D_MODEL), lambda e, bs, br: (layer, e, 0, 0)),
            ],
            out_specs=pl.BlockSpec(memory_space=pl.ANY),
            scratch_shapes=[
                pltpu.VMEM((D_MODEL, 2 * D_EXPERT), BF16),
                pltpu.VMEM((D_EXPERT, D_MODEL), BF16),
                pltpu.VMEM((rows, XS_COLS), BF16),
                pltpu.VMEM((rows, D_MODEL), BF16),
                pltpu.SemaphoreType.DMA(()),
                pltpu.SemaphoreType.DMA(()),
            ],
        ),
        out_shape=jax.ShapeDtypeStruct(xs.shape, xs.dtype),
        input_output_aliases={2: 0},
        compiler_params=_cparams(("arbitrary",)),
        name="moe_experts",
    )(bstart, bsrc, xs, w_e_in, w_e_out)


def _combine_kernel(used_ref, ys_ref, sel_ref, h2_ref, x1_ref, mod_ref, wsi_ref, wso_ref, lng_ref, lnb_ref,
                    o_ref, acc_scr, rankt_scr, row_scr, *, alpha):
    s = pl.program_id(0)
    rc = pl.program_id(1)
    n = h2_ref.shape[0]

    @pl.when(rc == 0)
    def _():
        rankp, _, boff_row, nb_row = _sorted_layout(sel_ref[...])
        ti = lax.broadcasted_iota(jnp.int32, (n, n), 0)
        tj = lax.broadcasted_iota(jnp.int32, (n, n), 1)
        eye = jnp.where(ti == tj, 1.0, 0.0).astype(BF16)
        rankt_scr[...] = _dot_nt(eye, rankp.astype(BF16)).astype(BF16)
        row_scr[0:1, :] = boff_row
        row_scr[1:2, :] = nb_row
        h = _dot(h2_ref[...], wsi_ref[...])
        act = (_silu(h[:, 0:D_SHARED]) * h[:, D_SHARED:2 * D_SHARED]).astype(BF16)
        acc_scr[...] = _dot(act, wso_ref[...])

    row0 = rc * COMB_ROWS

    @pl.when(row0 < used_ref[s])
    def _():
        boff_row = row_scr[0:1, :]
        onehot_e = _row_expert_onehot(row0, COMB_ROWS, boff_row, row_scr[1:2, :])
        q1 = _dot_nt(rankt_scr[...], onehot_e)
        kb = _dot_nt(jnp.broadcast_to(boff_row, (n, N_EXPERTS)).astype(BF16), onehot_e)
        krow = (lax.broadcasted_iota(jnp.int32, (n, COMB_ROWS), 1) + (row0 + 1)).astype(F32) - ROW_BLK * kb
        c = jnp.where(q1 == krow, 1.0, 0.0).astype(BF16)
        acc_scr[...] += _dot(c, ys_ref[...])

    @pl.when(rc == pl.num_programs(1) - 1)
    def _():
        g2 = mod_ref[:, 5 * D_MODEL:6 * D_MODEL]
        y = _layer_norm(alpha * x1_ref[...] + g2 * acc_scr[...])
        o_ref[...] = y * lng_ref[1:2, :] + lnb_ref[1:2, :]


def _combine(used_rows, ys, sel_t, h2, x1, mod_l, w_sh_in_l, w_sh_out_l, ln_g_l, ln_b_l, alpha, cond_of_tile):
    t = h2.shape[0]
    nt = t // MOE_TILE
    nrc = TILE_ROWS // COMB_ROWS
    nc = mod_l.shape[0]
    kern = functools.partial(_combine_kernel, alpha=alpha)
    row = lambda s, r, u: (s, 0)
    const = lambda s, r, u: (0, 0)
    return pl.pallas_call(
        kern,
        grid_spec=pltpu.PrefetchScalarGridSpec(
            num_scalar_prefetch=1,
            grid=(nt, nrc),
            in_specs=[
                pl.BlockSpec((COMB_ROWS, D_MODEL), lambda s, r, u: (s * nrc + r, 0)),
                pl.BlockSpec((N_EXPERTS, MOE_TILE), lambda s, r, u: (0, s)),
                pl.BlockSpec((MOE_TILE, D_MODEL), row),
                pl.BlockSpec((MOE_TILE, D_MODEL), row),
                pl.BlockSpec((None, 1, 6 * D_MODEL), lambda s, r, u: (cond_of_tile(s), 0, 0)),
                pl.BlockSpec((D_MODEL, 2 * D_SHARED), const),
                pl.BlockSpec((D_SHARED, D_MODEL), const),
                pl.BlockSpec((2, D_MODEL), const),
                pl.BlockSpec((2, D_MODEL), const),
            ],
            out_specs=pl.BlockSpec((MOE_TILE, D_MODEL), row),
            scratch_shapes=[
                pltpu.VMEM((MOE_TILE, D_MODEL), F32),
                pltpu.VMEM((MOE_TILE, N_EXPERTS), BF16),
                pltpu.VMEM((8, N_EXPERTS), F32),
            ],
        ),
        out_shape=jax.ShapeDtypeStruct((t, D_MODEL), F32),
        compiler_params=_cparams(("arbitrary", "arbitrary")),
        name="moe_combine",
    )(used_rows, ys, sel_t, h2, x1, mod_l.reshape(nc, 1, 6 * D_MODEL), w_sh_in_l, w_sh_out_l, ln_g_l, ln_b_l)


def _block_lists(nb):
    nt = nb.shape[0]
    cum = jnp.cumsum(nb, axis=1)
    boff = cum - nb
    used_blk = cum[:, -1]
    per_e = jnp.sum(nb, axis=0)
    bstart = jnp.concatenate([jnp.zeros((1,), jnp.int32), jnp.cumsum(per_e).astype(jnp.int32)])
    pref = jnp.cumsum(nb, axis=0) - nb
    lb = jnp.arange(TILE_BLKS, dtype=jnp.int32)
    e_of = jnp.sum((cum[:, None, :] <= lb[None, :, None]).astype(jnp.int32), axis=-1)
    e_cl = jnp.minimum(e_of, N_EXPERTS - 1)
    valid = lb[None, :] < used_blk[:, None]
    take = lambda a: jnp.take_along_axis(a, e_cl, axis=1)
    pos = bstart[e_cl] + take(pref) + (lb[None, :] - take(boff))
    total = nt * TILE_BLKS
    pos = jnp.where(valid, pos, total)
    src = (jnp.arange(nt, dtype=jnp.int32)[:, None] * TILE_BLKS + lb[None, :]).reshape(-1)
    bsrc = jnp.zeros((total,), jnp.int32).at[pos.reshape(-1)].set(src, mode="drop")
    return (used_blk * ROW_BLK).astype(jnp.int32), bstart, bsrc


def _lower_bound_consts(lb_raw):
    p = jax.nn.softmax(lb_raw.astype(F32), axis=0)
    cs = jnp.cumsum(p, axis=0)
    lb = jnp.clip(jnp.concatenate([jnp.zeros_like(cs[:1]), cs[:-1]], axis=0), 0.0, LB_MAX)
    rows = jnp.stack([jnp.log(lb[:, 0]), jnp.log1p(-lb[:, 0]), 1.0 - lb[:, 0],
                      jnp.log(lb[:, 1]), jnp.log1p(-lb[:, 1]), 1.0 - lb[:, 1]], axis=1)
    return jnp.concatenate([rows, jnp.zeros((rows.shape[0], 2, W_R), F32)], axis=1)


def _rope_tables(seq_len):
    t = jnp.arange(seq_len)
    half = HEAD_DIM // 2
    nf = half // 2
    inv = ROPE_THETA ** (-jnp.arange(nf, dtype=F32) / nf)

    def tabs(pos):
        ang = pos[:, None].astype(F32) * inv[None, :]
        cos = jnp.concatenate([jnp.cos(ang), jnp.cos(ang)], -1)
        sin = jnp.sin(ang)
        zero = jnp.zeros_like(sin)
        return cos, jnp.concatenate([-sin, zero], -1), jnp.concatenate([zero, sin], -1)

    row = tabs(t // GRID_W)
    colt = tabs(t % GRID_W)
    return tuple(jnp.concatenate([a, b], -1) for a, b in zip(row, colt))


def _bias_tables(rpb):
    cq = jnp.arange(GRID_W)
    cs = jnp.clip(cq - WIN_C // 2, 0, GRID_W - WIN_C)
    valid = (cq[None, :] >= cs[:, None]) & (cq[None, :] < cs[:, None] + WIN_C)
    coff = jnp.clip(cq[None, :] - cq[:, None], -(WIN_C - 1), WIN_C - 1) + WIN_C - 1
    bias_c = rpb.astype(F32)[:, :, :, coff]
    roff = jnp.arange(WIN_R)[:, None] + jnp.arange(WIN_R)[None, :]
    tab = bias_c[:, :, roff]
    tab = jnp.where(valid[None, None, None, None], tab, NEG_BIG)
    tab = tab.transpose(0, 1, 2, 4, 3, 5)
    return tab.reshape(tab.shape[0], H_A, WIN_R, GRID_W, WIN_R * GRID_W)


def kernel(x_prompt, x_sample, cache_k, cache_v, state_hgrn, c, c_ctx, w_ada, b_ada, w_in, w_out, lb_raw, hgrn_norm,
           rpb, ln_g, ln_b, w_router, router_bias, w_e_in, w_e_out, w_sh_in, w_sh_out):
    n_p, seq, d = x_prompt.shape
    n_s, dseq, _ = x_sample.shape
    depth = w_in.shape[0]
    t_p, t_s = n_p * seq, n_s * dseq
    t = t_p + t_s
    alpha = (2 * depth) ** 0.25
    assert d == D_MODEL and seq % MOE_TILE == 0 and dseq % MOE_TILE == 0 and t_p % dseq == 0
    assert dseq % GRID_W == 0 and dseq // GRID_W >= WIN_R

    tm_in = next(m for m in (1024, 512, 256) if t_p % m == 0 and dseq % m == 0)

    def cond_of_tile(tile_rows):
        n_p_tiles = t_p // tile_rows
        per_seq = dseq // tile_rows
        return lambda i: jnp.where(i < n_p_tiles, 0, 1 + (i - n_p_tiles) // per_seq)

    n_cond = -(-(1 + n_s) // 8) * 8
    cond = jnp.zeros((n_cond, d), F32).at[0].set(c_ctx).at[1:1 + n_s].set(c)
    mod = _modulation(cond, w_ada, b_ada)

    lbc = _lower_bound_consts(lb_raw)
    rope_tabs = _rope_tables(dseq)
    bias_tabs = _bias_tables(rpb)
    w_in_b = w_in.astype(BF16)
    w_out_b = w_out.astype(BF16)
    w_sh_in_b = w_sh_in.astype(BF16)
    w_sh_out_b = w_sh_out.astype(BF16)
    wr_t = jnp.swapaxes(w_router, 1, 2).astype(BF16)

    x = jnp.concatenate([x_prompt.reshape(t_p, d), x_sample.reshape(t_s, d)], axis=0)
    new_k, new_v, new_s = [], [], []
    for l in range(depth):
        proj = _inproj(x, mod[l], w_in_b[l], tm_in, cond_of_tile(tm_in))
        nrm = hgrn_norm[l].reshape(1, HEAD_DIM)
        or_p, st_p = _hgrn(proj, lbc[l], nrm, seq, n_p, 0, emit_state=True)
        (or_s,) = _hgrn(proj, lbc[l], nrm, dseq, n_s, t_p // dseq, state_in=state_hgrn, layer=l)
        oa_p, k_l, v_l = _ctx_attention(proj, seq, n_p)
        oa_s = _nbr_attention(proj, cache_k, cache_v, bias_tabs[l], rope_tabs, l, dseq, n_s, t_p // dseq)
        o_r = jnp.concatenate([or_p, or_s], axis=0)
        o_a = jnp.concatenate([oa_p, oa_s], axis=0)
        x1, h2, w_t, sel_t, nb = _outproj(o_r, o_a, x, mod[l], w_out_b[l], ln_g[l], ln_b[l], wr_t[l],
                                          router_bias[l].reshape(N_EXPERTS, 1), alpha, cond_of_tile(MOE_TILE))
        used_rows, bstart, bsrc = _block_lists(nb[:, 0, :].astype(jnp.int32))
        xs = _dispatch(used_rows, h2, sel_t, w_t)
        ys = _experts(bstart, bsrc, xs, w_e_in, w_e_out, l)
        x = _combine(used_rows, ys, sel_t, h2, x1, mod[l], w_sh_in_b[l], w_sh_out_b[l], ln_g[l], ln_b[l], alpha,
                     cond_of_tile(MOE_TILE))
        new_k.append(k_l)
        new_v.append(v_l)
        new_s.append(st_p)
    return (x[:t_p].reshape(n_p, seq, d), x[t_p:].reshape(n_s, dseq, d),
            jnp.stack(new_k, axis=1), jnp.stack(new_v, axis=1), jnp.stack(new_s, axis=1))
```

```python
import functools

import jax
import jax.numpy as jnp
from jax import lax
from jax.experimental import pallas as pl
from jax.experimental.pallas import tpu as pltpu

F32 = jnp.float32
BF16 = jnp.bfloat16

D_MODEL = 2048
HEAD_DIM = 128
N_HEADS = D_MODEL // HEAD_DIM
H_R = N_HEADS // 2
H_A = N_HEADS - H_R
W_R = H_R * HEAD_DIM
W_A = H_A * HEAD_DIM
N_COLS = 5 * W_R + 3 * W_A
N_COLBLK = N_COLS // HEAD_DIM
CHUNK = 64
SUB = 16
GRID_W = 64
WIN_R = 8
WIN_C = 16
ROPE_THETA = 10000.0
N_EXPERTS = 64
TOP_K = 8
N_GROUPS = 8
GROUP_SIZE = N_EXPERTS // N_GROUPS
TOPK_GROUPS = 4
D_EXPERT = 512
D_SHARED = 512
ROUTED_SCALE = 2.5
EPS = 1e-6
LB_MAX = 1.0 - 1e-4
NEG_BIG = -1e30
EXP_CLAMP = 60.0

MOE_TILE = 256
ROW_BLK = 16
TILE_ROWS = MOE_TILE * TOP_K + N_EXPERTS * ROW_BLK
TILE_BLKS = TILE_ROWS // ROW_BLK
W_LANES = 128
XS_COLS = D_MODEL + W_LANES
DISP_ROWS = 512
COMB_ROWS = 768
EXP_GROUP = 32
VMEM_LIMIT = 52 * 1024 * 1024


def _cparams(sem):
    return pltpu.CompilerParams(dimension_semantics=sem, vmem_limit_bytes=VMEM_LIMIT)


def _dot(a, b):
    return jnp.dot(a, b, preferred_element_type=F32)


def _dot_nt(a, b):
    return lax.dot_general(a, b, (((1,), (1,)), ((), ())), preferred_element_type=F32)


def _dot_tn(a, b):
    return lax.dot_general(a, b, (((0,), (0,)), ((), ())), preferred_element_type=F32)


def _silu(x):
    return x * jax.nn.sigmoid(x)


def _layer_norm(x):
    mu = jnp.mean(x, axis=-1, keepdims=True)
    xc = x - mu
    var = jnp.mean(xc * xc, axis=-1, keepdims=True)
    return xc * lax.rsqrt(var + EPS)


def _split3(x):
    a = x.astype(BF16)
    r = x - a.astype(F32)
    b = r.astype(BF16)
    c = (r - b.astype(F32)).astype(BF16)
    return a, b, c


def _mod_kernel(c_ref, w_ref, b_ref, o_ref):
    a = _silu(c_ref[...]).astype(BF16)
    o_ref[...] = _dot(a, w_ref[...].astype(BF16)) + b_ref[...]


def _modulation(cond, w_ada, b_ada):
    depth, d, n6 = w_ada.shape
    nc = cond.shape[0]
    tn = 1024
    return pl.pallas_call(
        _mod_kernel,
        grid=(depth, n6 // tn),
        in_specs=[
            pl.BlockSpec((nc, d), lambda l, j: (0, 0)),
            pl.BlockSpec((None, d, tn), lambda l, j: (l, 0, j)),
            pl.BlockSpec((None, 1, tn), lambda l, j: (l, 0, j)),
        ],
        out_specs=pl.BlockSpec((None, nc, tn), lambda l, j: (l, 0, j)),
        out_shape=jax.ShapeDtypeStruct((depth, nc, n6), F32),
        compiler_params=_cparams(("arbitrary", "arbitrary")),
        name="modulation",
    )(cond, w_ada, b_ada.reshape(depth, 1, n6))


def _inproj_kernel(x_ref, mod_ref, w_ref, o_ref, h_scr):
    @pl.when(pl.program_id(1) == 0)
    def _():
        y = _layer_norm(x_ref[...])
        sh = mod_ref[:, 0:D_MODEL]
        sc = mod_ref[:, D_MODEL:2 * D_MODEL]
        h_scr[...] = (y * (1.0 + sc) + sh).astype(BF16)

    res = _dot(h_scr[...], w_ref[...]).astype(BF16)
    for j in range(o_ref.shape[0]):
        o_ref[j] = res[:, j * HEAD_DIM:(j + 1) * HEAD_DIM]


def _inproj(x, mod_l, w_in_l, tm, cond_of_tile):
    t = x.shape[0]
    tn = 1024
    nblk = tn // HEAD_DIM
    nc = mod_l.shape[0]
    return pl.pallas_call(
        _inproj_kernel,
        grid=(t // tm, N_COLS // tn),
        in_specs=[
            pl.BlockSpec((tm, D_MODEL), lambda i, j: (i, 0)),
            pl.BlockSpec((None, 1, 6 * D_MODEL), lambda i, j: (cond_of_tile(i), 0, 0)),
            pl.BlockSpec((D_MODEL, tn), lambda i, j: (0, j)),
        ],
        out_specs=pl.BlockSpec((nblk, tm, HEAD_DIM), lambda i, j: (j, i, 0)),
        out_shape=jax.ShapeDtypeStruct((N_COLBLK, t, HEAD_DIM), BF16),
        scratch_shapes=[pltpu.VMEM((tm, D_MODEL), BF16)],
        compiler_params=_cparams(("arbitrary", "arbitrary")),
        name="inproj",
    )(x, mod_l.reshape(nc, 1, 6 * D_MODEL), w_in_l)


def _gates(z, log_lb, log1m_lb, one_m_lb):
    lse = jnp.log(1.0 + jnp.exp(-jnp.abs(z)))
    log_sig = jnp.minimum(z, 0.0) - lse
    k = one_m_lb * jnp.exp(log_sig - z)
    c = log1m_lb + log_sig
    g = jnp.maximum(log_lb, c) + jnp.log(1.0 + jnp.exp(-jnp.abs(log_lb - c)))
    return k, g


def _chunk_cumsum(tri, g):
    r = _dot(tri, jnp.concatenate(_split3(g), axis=1))
    return r[:, 0:HEAD_DIM] + r[:, HEAD_DIM:2 * HEAD_DIM] + r[:, 2 * HEAD_DIM:3 * HEAD_DIM]


def _intra_scores(q, k, b, reverse):
    zeros = jnp.zeros((SUB, HEAD_DIM), F32)
    nsub = CHUNK // SUB
    lhs_rows, rhs_slots = [], []
    for blk in range(nsub):
        lo = blk * SUB
        mid = lo + SUB // 2 if reverse else lo + SUB // 2 - 1
        r = b[mid:mid + 1, :]
        qe = q[lo:lo + SUB] * jnp.exp(jnp.minimum(b[lo:lo + SUB] - r, EXP_CLAMP))
        lhs_rows.append(jnp.concatenate([qe if j == blk else zeros for j in range(nsub)], axis=1))
        rhs_slots.append(k * jnp.exp(jnp.minimum(r - b, EXP_CLAMP)))
    lhs = jnp.concatenate(lhs_rows, axis=0).astype(BF16)
    rhs = jnp.concatenate(rhs_slots, axis=1).astype(BF16)
    return _dot_nt(lhs, rhs)


def _hgrn_kernel(*refs, seq_len, has_init, emit_state):
    q_ref, i_ref, zf_ref, zb_ref, g_ref, lbc_ref, nrm_ref = refs[:7]
    pos = 7
    s0_ref = None
    if has_init:
        s0_ref = refs[pos]
        pos += 1
    o_ref = refs[pos]
    pos += 1
    so_ref = None
    if emit_state:
        so_ref = refs[pos]
        pos += 1
    o_scr, qd_scr, ut_scr, dec_scr = refs[pos:pos + 4]

    n_chunks = seq_len // CHUNK
    ri = lax.broadcasted_iota(jnp.int32, (CHUNK, CHUNK), 0)
    ci = lax.broadcasted_iota(jnp.int32, (CHUNK, CHUNK), 1)
    tri_f = jnp.where(ci <= ri, 1.0, 0.0).astype(BF16)
    tri_b = jnp.where(ci >= ri, 1.0, 0.0).astype(BF16)
    lbc = lbc_ref[...]
    lbc_f = (lbc[0:1], lbc[1:2], lbc[2:3])
    lbc_b = (lbc[3:4], lbc[4:5], lbc[5:6])

    if has_init:
        st_f0 = s0_ref[0].T
        st_b0 = s0_ref[1].T
    else:
        st_f0 = jnp.zeros((HEAD_DIM, HEAD_DIM), F32)
        st_b0 = jnp.zeros((HEAD_DIM, HEAD_DIM), F32)

    per_it = 2
    dirs = ((zf_ref, lbc_f, tri_f, ci <= ri, False), (zb_ref, lbc_b, tri_b, ci >= ri, True))

    def local(it, carry):
        chains = []
        for u in range(per_it):
            c = it * per_it + u
            rows = pl.ds(pl.multiple_of(c * CHUNK, CHUNK), CHUNK)
            q = _silu(q_ref[rows, :].astype(F32))
            v = i_ref[rows, :]
            for d, (z_ref, lbc_d, tri, keep, rev) in enumerate(dirs):
                k, g = _gates(z_ref[rows, :].astype(F32), lbc_d[0], lbc_d[1], lbc_d[2])
                chains.append(dict(c=c, rows=rows, d=d, q=q, v=v, k=k, g=g, tri=tri, keep=keep, rev=rev))
        for ch in chains:
            ch["b"] = _chunk_cumsum(ch["tri"], ch["g"])
        for ch in chains:
            ch["a"] = _intra_scores(ch["q"], ch["k"], ch["b"], ch["rev"])
        for ch in chains:
            d, rows, b = ch["d"], ch["rows"], ch["b"]
            a = jnp.where(ch["keep"], ch["a"], 0.0).astype(BF16)
            o_scr[d, rows, :] = _dot(a, ch["v"])
            qd_scr[d, rows, :] = (ch["q"] * jnp.exp(b)).astype(BF16)
            edge = b[0:1, :] if ch["rev"] else b[CHUNK - 1:CHUNK, :]
            ut_scr[d, ch["c"]] = _dot_tn(ch["v"], (ch["k"] * jnp.exp(edge - b)).astype(BF16))
            dec_scr[d, ch["c"]] = jnp.exp(edge)
        return carry

    lax.fori_loop(0, n_chunks // per_it, local, 0)

    def scan(j, carry):
        new = []
        for d, st in enumerate(carry):
            c = j if d == 0 else n_chunks - 1 - j
            rows = pl.ds(pl.multiple_of(c * CHUNK, CHUNK), CHUNK)
            o_scr[d, rows, :] += _dot_nt(qd_scr[d, rows, :], st.astype(BF16))
            new.append(st * dec_scr[d, c] + ut_scr[d, c])
        return tuple(new)

    st_f, st_b = lax.fori_loop(0, n_chunks, scan, (st_f0, st_b0), unroll=2)
    if emit_state:
        so_ref[0] = st_f.T
        so_ref[1] = st_b.T

    nrm = nrm_ref[...]
    piece = 256 if seq_len % 256 == 0 else CHUNK

    def fin(p, carry):
        r0 = pl.multiple_of(p * piece, piece)
        o = o_scr[0, pl.ds(r0, piece), :] + o_scr[1, pl.ds(r0, piece), :]
        o = o * lax.rsqrt(jnp.mean(o * o, axis=-1, keepdims=True) + EPS) * nrm
        gate = g_ref[pl.ds(r0, piece), :].astype(F32)
        o_ref[pl.ds(r0, piece), :] = (o * _silu(gate)).astype(BF16)
        return carry

    lax.fori_loop(0, seq_len // piece, fin, 0)


def _hgrn(proj, lbc, nrm, seq_len, n_seq, row_blk0, state_in=None, layer=0, emit_state=False):
    kern = functools.partial(_hgrn_kernel, seq_len=seq_len, has_init=state_in is not None, emit_state=emit_state)

    def col(sec):
        return pl.BlockSpec((None, seq_len, HEAD_DIM), lambda b, h: (sec * H_R + h, row_blk0 + b, 0))

    in_specs = [col(0), col(1), col(2), col(3), col(4),
                pl.BlockSpec((8, HEAD_DIM), lambda b, h: (0, h)),
                pl.BlockSpec((1, HEAD_DIM), lambda b, h: (0, 0))]
    args = [proj, proj, proj, proj, proj, lbc, nrm]
    if state_in is not None:
        in_specs.append(pl.BlockSpec((None, None, 2, None, HEAD_DIM, HEAD_DIM),
                                     lambda b, h: (b, layer, 0, h, 0, 0)))
        args.append(state_in)
    out_specs = [pl.BlockSpec((seq_len, HEAD_DIM), lambda b, h: (b, h))]
    out_shape = [jax.ShapeDtypeStruct((n_seq * seq_len, W_R), BF16)]
    if emit_state:
        out_specs.append(pl.BlockSpec((None, 2, None, HEAD_DIM, HEAD_DIM), lambda b, h: (b, 0, h, 0, 0)))
        out_shape.append(jax.ShapeDtypeStruct((n_seq, 2, H_R, HEAD_DIM, HEAD_DIM), F32))
    res = pl.pallas_call(
        kern,
        grid=(n_seq, H_R),
        in_specs=in_specs,
        out_specs=out_specs,
        out_shape=out_shape,
        scratch_shapes=[pltpu.VMEM((2, seq_len, HEAD_DIM), F32),
                        pltpu.VMEM((2, seq_len, HEAD_DIM), BF16),
                        pltpu.VMEM((2, seq_len // CHUNK, HEAD_DIM, HEAD_DIM), F32),
                        pltpu.VMEM((2, seq_len // CHUNK, 1, HEAD_DIM), F32)],
        compiler_params=_cparams(("arbitrary", "arbitrary")),
        name="hgrn_state" if emit_state else "hgrn",
    )(*args)
    return res


def _ctx_attn_kernel(q_ref, k_ref, v_ref, o_ref, ko_ref, vo_ref):
    q = q_ref[...]
    k = k_ref[...]
    v = v_ref[...]
    s = _dot_nt(q, k) * (HEAD_DIM ** -0.5)
    m = jnp.max(s, axis=-1, keepdims=True)
    p = jnp.exp(s - m)
    den = jnp.sum(p, axis=-1, keepdims=True)
    o = _dot(p.astype(BF16), v) / den
    o_ref[...] = o.astype(BF16)
    ko_ref[...] = k.astype(F32)
    vo_ref[...] = v.astype(F32)


def _ctx_attention(proj, seq_len, n_seq):
    def col(sec):
        return pl.BlockSpec((None, seq_len, HEAD_DIM), lambda b, h: (5 * H_R + sec * H_A + h, b, 0))

    cache_spec = pl.BlockSpec((None, None, seq_len, HEAD_DIM), lambda b, h: (b, h, 0, 0))
    cache_shape = jax.ShapeDtypeStruct((n_seq, H_A, seq_len, HEAD_DIM), F32)
    return pl.pallas_call(
        _ctx_attn_kernel,
        grid=(n_seq, H_A),
        in_specs=[col(0), col(1), col(2)],
        out_specs=[pl.BlockSpec((seq_len, HEAD_DIM), lambda b, h: (b, h)), cache_spec, cache_spec],
        out_shape=[jax.ShapeDtypeStruct((n_seq * seq_len, W_A), BF16), cache_shape, cache_shape],
        compiler_params=_cparams(("arbitrary", "arbitrary")),
        name="ctx_attention",
    )(proj, proj, proj)


def _rope(x, cos, sin_lo, sin_hi):
    return (x * cos + pltpu.roll(x, HEAD_DIM - HEAD_DIM // 4, axis=1) * sin_lo
            + pltpu.roll(x, HEAD_DIM // 4, axis=1) * sin_hi)


def _nbr_attn_kernel(q_ref, k_ref, v_ref, kc_ref, vc_ref, bias_ref, cos_ref, slo_ref, shi_ref, o_ref, kr_scr,
                     *, n_rows):
    rows_pc = 256

    def rope_k(p, carry):
        r0 = pl.multiple_of(p * rows_pc, rows_pc)
        sl = pl.ds(r0, rows_pc)
        kr_scr[sl, :] = _rope(k_ref[sl, :].astype(F32), cos_ref[sl, :], slo_ref[sl, :], shi_ref[sl, :]).astype(BF16)
        return carry

    lax.fori_loop(0, (n_rows * GRID_W) // rows_pc, rope_k, 0)
    kc = kc_ref[...].astype(BF16)
    vc = vc_ref[...].astype(BF16)
    scale = HEAD_DIM ** -0.5
    win = WIN_R * GRID_W

    per_it = 4 if n_rows % 4 == 0 else 1

    def rows_step(it, carry):
        items = []
        for u in range(per_it):
            r = it * per_it + u
            rs = jnp.clip(r - WIN_R // 2, 0, n_rows - WIN_R)
            qs = pl.ds(pl.multiple_of(r * GRID_W, GRID_W), GRID_W)
            ks = pl.ds(pl.multiple_of(rs * GRID_W, GRID_W), win)
            q = _rope(q_ref[qs, :].astype(F32), cos_ref[qs, :], slo_ref[qs, :], shi_ref[qs, :])
            items.append(dict(qs=qs, ks=ks, dl=rs - r + WIN_R - 1, q=(q * scale).astype(BF16)))
        for x in items:
            x["s_loc"] = _dot_nt(x["q"], kr_scr[x["ks"], :]) + bias_ref[x["dl"]]
            x["s_ctx"] = _dot_nt(x["q"], kc)
        for x in items:
            m = jnp.maximum(jnp.max(x["s_loc"], axis=-1, keepdims=True), jnp.max(x["s_ctx"], axis=-1, keepdims=True))
            p_loc = jnp.exp(x["s_loc"] - m)
            p_ctx = jnp.exp(x["s_ctx"] - m)
            x["den"] = jnp.sum(p_loc, axis=-1, keepdims=True) + jnp.sum(p_ctx, axis=-1, keepdims=True)
            x["p_loc"] = p_loc.astype(BF16)
            x["p_ctx"] = p_ctx.astype(BF16)
        for x in items:
            o = _dot(x["p_loc"], v_ref[x["ks"], :]) + _dot(x["p_ctx"], vc)
            o_ref[x["qs"], :] = (o / x["den"]).astype(BF16)
        return carry

    lax.fori_loop(0, n_rows // per_it, rows_step, 0)


def _nbr_attention(proj, cache_k, cache_v, bias_tab, rope_tabs, layer, seq_len, n_seq, row_blk0):
    n_rows = seq_len // GRID_W
    past = cache_k.shape[3]
    kern = functools.partial(_nbr_attn_kernel, n_rows=n_rows)

    def col(sec):
        return pl.BlockSpec((None, seq_len, HEAD_DIM), lambda b, h: (5 * H_R + sec * H_A + h, row_blk0 + b, 0))

    cache_spec = pl.BlockSpec((None, None, None, past, HEAD_DIM), lambda b, h: (b, layer, h, 0, 0))
    tab_spec = pl.BlockSpec((seq_len, HEAD_DIM), lambda b, h: (0, 0))
    return pl.pallas_call(
        kern,
        grid=(n_seq, H_A),
        in_specs=[col(0), col(1), col(2), cache_spec, cache_spec,
                  pl.BlockSpec((None, WIN_R, GRID_W, WIN_R * GRID_W), lambda b, h: (h, 0, 0, 0)),
                  tab_spec, tab_spec, tab_spec],
        out_specs=pl.BlockSpec((seq_len, HEAD_DIM), lambda b, h: (b, h)),
        out_shape=jax.ShapeDtypeStruct((n_seq * seq_len, W_A), BF16),
        scratch_shapes=[pltpu.VMEM((seq_len, HEAD_DIM), BF16)],
        compiler_params=_cparams(("arbitrary", "arbitrary")),
        name="nbr_attention",
    )(proj, proj, proj, cache_k, cache_v, bias_tab, *rope_tabs)


def _first_argmax_rows(vals, ids, sentinel):
    m = vals[0]
    for v in vals[1:]:
        m = jnp.maximum(m, v)
    m = jnp.max(m, axis=0, keepdims=True)
    best = None
    for v, i in zip(vals, ids):
        cand = jnp.min(jnp.where(v == m, i, sentinel), axis=0, keepdims=True)
        best = cand if best is None else jnp.minimum(best, cand)
    return m, best


def _route(s, sb):
    n = s.shape[1]
    iota8 = lax.broadcasted_iota(jnp.int32, (GROUP_SIZE, n), 0)
    neg_inf = -jnp.inf
    groups = [sb[g * GROUP_SIZE:(g + 1) * GROUP_SIZE, :] for g in range(N_GROUPS)]
    gscore = jnp.zeros((N_GROUPS, n), F32)
    for g, xg in enumerate(groups):
        m1, i1 = _first_argmax_rows([xg], [iota8], GROUP_SIZE)
        m2 = jnp.max(jnp.where(iota8 == i1, neg_inf, xg), axis=0, keepdims=True)
        gscore = jnp.where(iota8 == g, m1 + m2, gscore)
    gsel = jnp.zeros((N_GROUPS, n), F32)
    cur = gscore
    for _ in range(TOPK_GROUPS):
        _, i = _first_argmax_rows([cur], [iota8], N_GROUPS)
        hit = iota8 == i
        gsel = jnp.where(hit, 1.0, gsel)
        cur = jnp.where(hit, neg_inf, cur)
    cur = [jnp.where(gsel[g:g + 1, :] > 0.5, xg, neg_inf) for g, xg in enumerate(groups)]
    ids = [iota8 + g * GROUP_SIZE for g in range(N_GROUPS)]
    sel = [jnp.zeros((GROUP_SIZE, n), jnp.bool_) for _ in range(N_GROUPS)]
    for _ in range(TOP_K):
        _, i = _first_argmax_rows(cur, ids, N_EXPERTS)
        hits = [idg == i for idg in ids]
        sel = [jnp.logical_or(a, h) for a, h in zip(sel, hits)]
        cur = [jnp.where(h, neg_inf, c) for c, h in zip(cur, hits)]
    return sel


def _outproj_kernel(or_ref, oa_ref, x_ref, mod_ref, w_ref, lng_ref, lnb_ref, wr_ref, rb_ref,
                    x1_ref, h2_ref, wt_ref, sel_ref, nb_ref, *, alpha):
    mix = _dot(or_ref[...], w_ref[0:W_R, :]) + _dot(oa_ref[...], w_ref[W_R:W_R + W_A, :])
    g1 = mod_ref[:, 2 * D_MODEL:3 * D_MODEL]
    x1 = _layer_norm(alpha * x_ref[...] + g1 * mix) * lng_ref[0:1, :] + lnb_ref[0:1, :]
    x1_ref[...] = x1
    sh2 = mod_ref[:, 3 * D_MODEL:4 * D_MODEL]
    sc2 = mod_ref[:, 4 * D_MODEL:5 * D_MODEL]
    h2 = (_layer_norm(x1) * (1.0 + sc2) + sh2).astype(BF16)
    h2_ref[...] = h2
    s = jax.nn.sigmoid(_dot_nt(wr_ref[...], h2))
    sel = _route(s, s + rb_ref[...])
    n = s.shape[1]
    wsel = [jnp.where(sel[g], s[g * GROUP_SIZE:(g + 1) * GROUP_SIZE, :], 0.0) for g in range(N_GROUPS)]
    tot = wsel[0]
    for w in wsel[1:]:
        tot = tot + w
    tot = jnp.sum(tot, axis=0, keepdims=True)
    for g in range(N_GROUPS):
        sl = slice(g * GROUP_SIZE, (g + 1) * GROUP_SIZE)
        wt_ref[sl, :] = wsel[g] / tot * ROUTED_SCALE
        sel_ref[sl, :] = jnp.where(sel[g], 1.0, 0.0)
    selv = sel_ref[...].astype(BF16)
    cnt = _dot_nt(jnp.ones((8, n), BF16), selv)
    nb_ref[...] = jnp.floor((cnt + (ROW_BLK - 1)) * (1.0 / ROW_BLK))


def _outproj(o_r, o_a, x, mod_l, w_out_l, ln_g_l, ln_b_l, wr_t, rbias, alpha, cond_of_tile):
    t = x.shape[0]
    tm = MOE_TILE
    nt = t // tm
    nc = mod_l.shape[0]
    kern = functools.partial(_outproj_kernel, alpha=alpha)
    row = lambda i: (i, 0)
    const = lambda i: (0, 0)
    return pl.pallas_call(
        kern,
        grid=(nt,),
        in_specs=[
            pl.BlockSpec((tm, W_R), row),
            pl.BlockSpec((tm, W_A), row),
            pl.BlockSpec((tm, D_MODEL), row),
            pl.BlockSpec((None, 1, 6 * D_MODEL), lambda i: (cond_of_tile(i), 0, 0)),
            pl.BlockSpec((W_R + W_A, D_MODEL), const),
            pl.BlockSpec((2, D_MODEL), const),
            pl.BlockSpec((2, D_MODEL), const),
            pl.BlockSpec((N_EXPERTS, D_MODEL), const),
            pl.BlockSpec((N_EXPERTS, 1), const),
        ],
        out_specs=[
            pl.BlockSpec((tm, D_MODEL), row),
            pl.BlockSpec((tm, D_MODEL), row),
            pl.BlockSpec((N_EXPERTS, tm), lambda i: (0, i)),
            pl.BlockSpec((N_EXPERTS, tm), lambda i: (0, i)),
            pl.BlockSpec((None, 8, N_EXPERTS), lambda i: (i, 0, 0)),
        ],
        out_shape=[
            jax.ShapeDtypeStruct((t, D_MODEL), F32),
            jax.ShapeDtypeStruct((t, D_MODEL), BF16),
            jax.ShapeDtypeStruct((N_EXPERTS, t), F32),
            jax.ShapeDtypeStruct((N_EXPERTS, t), F32),
            jax.ShapeDtypeStruct((nt, 8, N_EXPERTS), F32),
        ],
        compiler_params=_cparams(("arbitrary",)),
        name="outproj_router",
    )(o_r, o_a, x, mod_l.reshape(nc, 1, 6 * D_MODEL), w_out_l, ln_g_l, ln_b_l, wr_t, rbias)


def _sorted_layout(sel):
    n = sel.shape[1]
    selb = sel.astype(BF16)
    ti = lax.broadcasted_iota(jnp.int32, (n, n), 0)
    tj = lax.broadcasted_iota(jnp.int32, (n, n), 1)
    rank = _dot(selb, jnp.where(ti < tj, 1.0, 0.0).astype(BF16))
    cnt = _dot(selb, jnp.ones((n, n), BF16))
    nb = jnp.floor((cnt + (ROW_BLK - 1)) * (1.0 / ROW_BLK))
    ei = lax.broadcasted_iota(jnp.int32, (N_EXPERTS, N_EXPERTS), 0)
    ej = lax.broadcasted_iota(jnp.int32, (N_EXPERTS, N_EXPERTS), 1)
    boff = _dot(jnp.where(ej < ei, 1.0, 0.0).astype(BF16), nb.astype(BF16))
    eye = lax.broadcasted_iota(jnp.int32, (N_EXPERTS, n), 0) == lax.broadcasted_iota(jnp.int32, (N_EXPERTS, n), 1)
    ones8 = jnp.ones((8, N_EXPERTS), BF16)
    boff_row = _dot(ones8, jnp.where(eye, boff, 0.0).astype(BF16))[0:1, 0:N_EXPERTS]
    nb_row = _dot(ones8, jnp.where(eye, nb, 0.0).astype(BF16))[0:1, 0:N_EXPERTS]
    rankp = jnp.where(sel > 0.5, rank + 1.0, 0.0)
    return rankp, boff, boff_row, nb_row


def _row_expert_onehot(row0, n_rows, boff_row, nb_row):
    blk = (lax.broadcasted_iota(jnp.int32, (n_rows, N_EXPERTS), 0) + row0) // ROW_BLK
    blk = blk.astype(F32)
    return jnp.where(jnp.logical_and(blk >= boff_row, blk < boff_row + nb_row), 1.0, 0.0).astype(BF16)


def _dispatch_kernel(used_ref, h2_ref, sel_ref, wt_ref, xs_ref, rank_scr, boff_scr, row_scr, w_scr):
    s = pl.program_id(0)
    rc = pl.program_id(1)

    @pl.when(rc == 0)
    def _():
        rankp, boff, boff_row, nb_row = _sorted_layout(sel_ref[...])
        rank_scr[...] = rankp.astype(BF16)
        boff_scr[...] = boff.astype(BF16)
        row_scr[0:1, :] = boff_row
        row_scr[1:2, :] = nb_row
        w1, w2, w3 = _split3(wt_ref[...])
        w_scr[0] = w1
        w_scr[1] = w2
        w_scr[2] = w3

    row0 = rc * DISP_ROWS

    @pl.when(row0 < used_ref[s])
    def _():
        n = h2_ref.shape[0]
        onehot_e = _row_expert_onehot(row0, DISP_ROWS, row_scr[0:1, :], row_scr[1:2, :])
        q1 = _dot(onehot_e, rank_scr[...])
        kb = _dot(onehot_e, boff_scr[...])
        krow = (lax.broadcasted_iota(jnp.int32, (DISP_ROWS, n), 0) + (row0 + 1)).astype(F32) - ROW_BLK * kb
        hit = q1 == krow
        xs_ref[:, 0:D_MODEL] = _dot(jnp.where(hit, 1.0, 0.0).astype(BF16), h2_ref[...]).astype(BF16)
        wq = _dot(onehot_e, w_scr[0]) + _dot(onehot_e, w_scr[1]) + _dot(onehot_e, w_scr[2])
        ws = jnp.sum(jnp.where(hit, wq, 0.0), axis=1, keepdims=True)
        p1, p2, p3 = (p.astype(F32) for p in _split3(ws))
        lane = lax.broadcasted_iota(jnp.int32, (DISP_ROWS, W_LANES), 1)
        wl = jnp.where(lane == 0, p1, jnp.where(lane == 1, p2, jnp.where(lane == 2, p3, 0.0)))
        xs_ref[:, D_MODEL:XS_COLS] = wl.astype(BF16)

    @pl.when(row0 >= used_ref[s])
    def _():
        xs_ref[...] = jnp.zeros(xs_ref.shape, BF16)


def _dispatch(used_rows, h2, sel_t, w_t):
    t = h2.shape[0]
    nt = t // MOE_TILE
    return pl.pallas_call(
        _dispatch_kernel,
        grid_spec=pltpu.PrefetchScalarGridSpec(
            num_scalar_prefetch=1,
            grid=(nt, TILE_ROWS // DISP_ROWS),
            in_specs=[
                pl.BlockSpec((MOE_TILE, D_MODEL), lambda s, r, u: (s, 0)),
                pl.BlockSpec((N_EXPERTS, MOE_TILE), lambda s, r, u: (0, s)),
                pl.BlockSpec((N_EXPERTS, MOE_TILE), lambda s, r, u: (0, s)),
            ],
            out_specs=pl.BlockSpec((DISP_ROWS, XS_COLS), lambda s, r, u: (s * (TILE_ROWS // DISP_ROWS) + r, 0)),
            scratch_shapes=[
                pltpu.VMEM((N_EXPERTS, MOE_TILE), BF16),
                pltpu.VMEM((N_EXPERTS, MOE_TILE), BF16),
                pltpu.VMEM((8, N_EXPERTS), F32),
                pltpu.VMEM((3, N_EXPERTS, MOE_TILE), BF16),
            ],
        ),
        out_shape=jax.ShapeDtypeStruct((nt * TILE_ROWS, XS_COLS), BF16),
        compiler_params=_cparams(("arbitrary", "arbitrary")),
        name="moe_dispatch",
    )(used_rows, h2, sel_t, w_t)


def _expert_kernel(bstart_ref, bsrc_ref, xs_hbm, w1_ref, w2_ref, ys_hbm, w1_scr, w2_scr, x_buf, y_buf, sem_in, sem_out):
    e = pl.program_id(0)
    b0 = bstart_ref[e]
    n_blk = bstart_ref[e + 1] - b0
    n_groups = (n_blk + EXP_GROUP - 1) // EXP_GROUP

    def in_copy(j, slot, k):
        src = pl.multiple_of(bsrc_ref[j] * ROW_BLK, ROW_BLK)
        row = pl.multiple_of(k * ROW_BLK, ROW_BLK)
        return pltpu.make_async_copy(xs_hbm.at[pl.ds(src, ROW_BLK), :],
                                     x_buf.at[slot, pl.ds(row, ROW_BLK), :], sem_in.at[slot])

    def out_copy(j, slot, k):
        dst = pl.multiple_of(bsrc_ref[j] * ROW_BLK, ROW_BLK)
        row = pl.multiple_of(k * ROW_BLK, ROW_BLK)
        return pltpu.make_async_copy(y_buf.at[slot, pl.ds(row, ROW_BLK), :],
                                     ys_hbm.at[pl.ds(dst, ROW_BLK), pl.ds(0, D_MODEL)], sem_out.at[slot])

    def for_blocks(gi, fn):
        k0 = gi * EXP_GROUP

        def body(k, carry):
            fn(b0 + k0 + k, k)
            return carry

        lax.fori_loop(0, jnp.minimum(n_blk - k0, EXP_GROUP), body, 0)

    @pl.when(e == 0)
    def _():
        x_buf[...] = jnp.zeros(x_buf.shape, BF16)

    @pl.when(n_groups > 0)
    def _():
        for_blocks(0, lambda j, k: in_copy(j, 0, k).start())

    w1_scr[...] = w1_ref[...].astype(BF16)
    w2_scr[...] = w2_ref[...].astype(BF16)

    def group(gi, carry):
        slot = gi % 2

        @pl.when(gi + 1 < n_groups)
        def _():
            for_blocks(gi + 1, lambda j, k: in_copy(j, 1 - slot, k).start())

        for_blocks(gi, lambda j, k: in_copy(j, slot, k).wait())

        @pl.when(gi >= 2)
        def _():
            for_blocks(gi - 2, lambda j, k: out_copy(j, slot, k).wait())

        x = x_buf[slot, :, 0:D_MODEL]
        wrow = jnp.sum(x_buf[slot, :, D_MODEL:XS_COLS].astype(F32), axis=1, keepdims=True)
        h = _dot(x, w1_scr[...])
        act = (_silu(h[:, 0:D_EXPERT]) * h[:, D_EXPERT:2 * D_EXPERT]).astype(BF16)
        y_buf[slot] = (_dot(act, w2_scr[...]) * wrow).astype(BF16)
        for_blocks(gi, lambda j, k: out_copy(j, slot, k).start())
        return carry

    lax.fori_loop(0, n_groups, group, 0)

    for back in (2, 1):
        @pl.when(n_groups >= back)
        def _():
            g_last = n_groups - back
            for_blocks(g_last, lambda j, k: out_copy(j, g_last % 2, k).wait())


def _experts(bstart, bsrc, xs, w_e_in, w_e_out, layer):
    rows = EXP_GROUP * ROW_BLK
    return pl.pallas_call(
        _expert_kernel,
        grid_spec=pltpu.PrefetchScalarGridSpec(
            num_scalar_prefetch=2,
            grid=(N_EXPERTS,),
            in_specs=[
                pl.BlockSpec(memory_space=pl.ANY),
                pl.BlockSpec((None, None, D_MODEL, 2 * D_EXPERT), lambda e, bs, br: (layer, e, 0, 0)),
                pl.BlockSpec((None, None, D_EXPERT, D_MODEL), lambda e, bs, br: (layer, e, 0, 0)),
            ],
            out_specs=pl.BlockSpec(memory_space=pl.ANY),
            scratch_shapes=[
                pltpu.VMEM((D_MODEL, 2 * D_EXPERT), BF16),
                pltpu.VMEM((D_EXPERT, D_MODEL), BF16),
                pltpu.VMEM((2, rows, XS_COLS), BF16),
                pltpu.VMEM((2, rows, D_MODEL), BF16),
                pltpu.SemaphoreType.DMA((2,)),
                pltpu.SemaphoreType.DMA((2,)),
            ],
        ),
        out_shape=jax.ShapeDtypeStruct(xs.shape, xs.dtype),
        input_output_aliases={2: 0},
        compiler_params=_cparams(("arbitrary",)),
        name="moe_experts",
    )(bstart, bsrc, xs, w_e_in, w_e_out)


def _combine_kernel(used_ref, ys_ref, sel_ref, h2_ref, x1_ref, mod_ref, wsi_ref, wso_ref, lng_ref, lnb_ref,
                    o_ref, acc_scr, rankt_scr, row_scr, *, alpha):
    s = pl.program_id(0)
    rc = pl.program_id(1)
    n = h2_ref.shape[0]

    @pl.when(rc == 0)
    def _():
        rankp, _, boff_row, nb_row = _sorted_layout(sel_ref[...])
        ti = lax.broadcasted_iota(jnp.int32, (n, n), 0)
        tj = lax.broadcasted_iota(jnp.int32, (n, n), 1)
        eye = jnp.where(ti == tj, 1.0, 0.0).astype(BF16)
        rankt_scr[...] = _dot_nt(eye, rankp.astype(BF16)).astype(BF16)
        row_scr[0:1, :] = boff_row
        row_scr[1:2, :] = nb_row
        h = _dot(h2_ref[...], wsi_ref[...])
        act = (_silu(h[:, 0:D_SHARED]) * h[:, D_SHARED:2 * D_SHARED]).astype(BF16)
        acc_scr[...] = _dot(act, wso_ref[...])

    row0 = rc * COMB_ROWS

    @pl.when(row0 < used_ref[s])
    def _():
        boff_row = row_scr[0:1, :]
        onehot_e = _row_expert_onehot(row0, COMB_ROWS, boff_row, row_scr[1:2, :])
        q1 = _dot_nt(rankt_scr[...], onehot_e)
        kb = _dot_nt(jnp.broadcast_to(boff_row, (n, N_EXPERTS)).astype(BF16), onehot_e)
        krow = (lax.broadcasted_iota(jnp.int32, (n, COMB_ROWS), 1) + (row0 + 1)).astype(F32) - ROW_BLK * kb
        c = jnp.where(q1 == krow, 1.0, 0.0).astype(BF16)
        acc_scr[...] += _dot(c, ys_ref[...])

    @pl.when(rc == pl.num_programs(1) - 1)
    def _():
        g2 = mod_ref[:, 5 * D_MODEL:6 * D_MODEL]
        y = _layer_norm(alpha * x1_ref[...] + g2 * acc_scr[...])
        o_ref[...] = y * lng_ref[1:2, :] + lnb_ref[1:2, :]


def _combine(used_rows, ys, sel_t, h2, x1, mod_l, w_sh_in_l, w_sh_out_l, ln_g_l, ln_b_l, alpha, cond_of_tile):
    t = h2.shape[0]
    nt = t // MOE_TILE
    nrc = TILE_ROWS // COMB_ROWS
    nc = mod_l.shape[0]
    kern = functools.partial(_combine_kernel, alpha=alpha)
    row = lambda s, r, u: (s, 0)
    const = lambda s, r, u: (0, 0)
    return pl.pallas_call(
        kern,
        grid_spec=pltpu.PrefetchScalarGridSpec(
            num_scalar_prefetch=1,
            grid=(nt, nrc),
            in_specs=[
                pl.BlockSpec((COMB_ROWS, D_MODEL), lambda s, r, u: (s * nrc + r, 0)),
                pl.BlockSpec((N_EXPERTS, MOE_TILE), lambda s, r, u: (0, s)),
                pl.BlockSpec((MOE_TILE, D_MODEL), row),
                pl.BlockSpec((MOE_TILE, D_MODEL), row),
                pl.BlockSpec((None, 1, 6 * D_MODEL), lambda s, r, u: (cond_of_tile(s), 0, 0)),
                pl.BlockSpec((D_MODEL, 2 * D_SHARED), const),
                pl.BlockSpec((D_SHARED, D_MODEL), const),
                pl.BlockSpec((2, D_MODEL), const),
                pl.BlockSpec((2, D_MODEL), const),
            ],
            out_specs=pl.BlockSpec((MOE_TILE, D_MODEL), row),
            scratch_shapes=[
                pltpu.VMEM((MOE_TILE, D_MODEL), F32),
                pltpu.VMEM((MOE_TILE, N_EXPERTS), BF16),
                pltpu.VMEM((8, N_EXPERTS), F32),
            ],
        ),
        out_shape=jax.ShapeDtypeStruct((t, D_MODEL), F32),
        compiler_params=_cparams(("arbitrary", "arbitrary")),
        name="moe_combine",
    )(used_rows, ys, sel_t, h2, x1, mod_l.reshape(nc, 1, 6 * D_MODEL), w_sh_in_l, w_sh_out_l, ln_g_l, ln_b_l)


def _block_lists(nb):
    nt = nb.shape[0]
    cum = jnp.cumsum(nb, axis=1)
    boff = cum - nb
    used_blk = cum[:, -1]
    per_e = jnp.sum(nb, axis=0)
    bstart = jnp.concatenate([jnp.zeros((1,), jnp.int32), jnp.cumsum(per_e).astype(jnp.int32)])
    pref = jnp.cumsum(nb, axis=0) - nb
    lb = jnp.arange(TILE_BLKS, dtype=jnp.int32)
    owner = (lb[None, :, None] >= boff[:, None, :]) & (lb[None, :, None] < cum[:, None, :])
    base = bstart[None, :-1] + pref - boff
    pos = jnp.sum(jnp.where(owner, base[:, None, :], 0), axis=-1) + lb[None, :]
    total = nt * TILE_BLKS
    pos = jnp.where(jnp.any(owner, axis=-1), pos, total)
    src = (jnp.arange(nt, dtype=jnp.int32)[:, None] * TILE_BLKS + lb[None, :]).reshape(-1)
    bsrc = jnp.zeros((total,), jnp.int32).at[pos.reshape(-1)].set(src, mode="drop")
    return (used_blk * ROW_BLK).astype(jnp.int32), bstart, bsrc


def _lower_bound_consts(lb_raw):
    p = jax.nn.softmax(lb_raw.astype(F32), axis=0)
    cs = jnp.cumsum(p, axis=0)
    lb = jnp.clip(jnp.concatenate([jnp.zeros_like(cs[:1]), cs[:-1]], axis=0), 0.0, LB_MAX)
    rows = jnp.stack([jnp.log(lb[:, 0]), jnp.log1p(-lb[:, 0]), 1.0 - lb[:, 0],
                      jnp.log(lb[:, 1]), jnp.log1p(-lb[:, 1]), 1.0 - lb[:, 1]], axis=1)
    return jnp.concatenate([rows, jnp.zeros((rows.shape[0], 2, W_R), F32)], axis=1)


def _rope_tables(seq_len):
    t = jnp.arange(seq_len)
    half = HEAD_DIM // 2
    nf = half // 2
    inv = ROPE_THETA ** (-jnp.arange(nf, dtype=F32) / nf)

    def tabs(pos):
        ang = pos[:, None].astype(F32) * inv[None, :]
        cos = jnp.concatenate([jnp.cos(ang), jnp.cos(ang)], -1)
        sin = jnp.sin(ang)
        zero = jnp.zeros_like(sin)
        return cos, jnp.concatenate([-sin, zero], -1), jnp.concatenate([zero, sin], -1)

    row = tabs(t // GRID_W)
    colt = tabs(t % GRID_W)
    return tuple(jnp.concatenate([a, b], -1) for a, b in zip(row, colt))


def _bias_tables(rpb):
    cq = jnp.arange(GRID_W)
    cs = jnp.clip(cq - WIN_C // 2, 0, GRID_W - WIN_C)
    valid = (cq[None, :] >= cs[:, None]) & (cq[None, :] < cs[:, None] + WIN_C)
    coff = jnp.clip(cq[None, :] - cq[:, None], -(WIN_C - 1), WIN_C - 1) + WIN_C - 1
    roff = jnp.arange(WIN_R)[:, None] + jnp.arange(WIN_R)[None, :]
    pick_c = (coff[None] == jnp.arange(2 * WIN_C - 1)[:, None, None]).astype(F32)
    pick_r = (roff[None] == jnp.arange(2 * WIN_R - 1)[:, None, None]).astype(F32)
    hi = lax.Precision.HIGHEST
    bias_c = jnp.einsum('dhrc,cqw->dhrqw', rpb.astype(F32), pick_c, precision=hi)
    tab = jnp.einsum('rlk,dhrqw->dhlqkw', pick_r, bias_c, precision=hi)
    tab = jnp.where(valid[None, None, None, :, None, :], tab, NEG_BIG)
    return tab.reshape(tab.shape[0], H_A, WIN_R, GRID_W, WIN_R * GRID_W)


def kernel(x_prompt, x_sample, cache_k, cache_v, state_hgrn, c, c_ctx, w_ada, b_ada, w_in, w_out, lb_raw, hgrn_norm,
           rpb, ln_g, ln_b, w_router, router_bias, w_e_in, w_e_out, w_sh_in, w_sh_out):
    n_p, seq, d = x_prompt.shape
    n_s, dseq, _ = x_sample.shape
    depth = w_in.shape[0]
    t_p, t_s = n_p * seq, n_s * dseq
    t = t_p + t_s
    alpha = (2 * depth) ** 0.25
    assert d == D_MODEL and seq % MOE_TILE == 0 and dseq % MOE_TILE == 0 and t_p % dseq == 0
    assert dseq % GRID_W == 0 and dseq // GRID_W >= WIN_R

    tm_in = next(m for m in (1024, 512, 256) if t_p % m == 0 and dseq % m == 0)

    def cond_of_tile(tile_rows):
        n_p_tiles = t_p // tile_rows
        per_seq = dseq // tile_rows
        return lambda i: jnp.where(i < n_p_tiles, 0, 1 + (i - n_p_tiles) // per_seq)

    n_cond = -(-(1 + n_s) // 8) * 8
    cond = jnp.zeros((n_cond, d), F32).at[0].set(c_ctx).at[1:1 + n_s].set(c)
    mod = _modulation(cond, w_ada, b_ada)

    lbc = _lower_bound_consts(lb_raw)
    rope_tabs = _rope_tables(dseq)
    bias_tabs = _bias_tables(rpb)
    w_in_b = w_in.astype(BF16)
    w_out_b = w_out.astype(BF16)
    w_sh_in_b = w_sh_in.astype(BF16)
    w_sh_out_b = w_sh_out.astype(BF16)
    wr_t = jnp.swapaxes(w_router, 1, 2).astype(BF16)

    x = jnp.concatenate([x_prompt.reshape(t_p, d), x_sample.reshape(t_s, d)], axis=0)
    new_k, new_v, new_s = [], [], []
    for l in range(depth):
        proj = _inproj(x, mod[l], w_in_b[l], tm_in, cond_of_tile(tm_in))
        nrm = hgrn_norm[l].reshape(1, HEAD_DIM)
        or_p, st_p = _hgrn(proj, lbc[l], nrm, seq, n_p, 0, emit_state=True)
        (or_s,) = _hgrn(proj, lbc[l], nrm, dseq, n_s, t_p // dseq, state_in=state_hgrn, layer=l)
        oa_p, k_l, v_l = _ctx_attention(proj, seq, n_p)
        oa_s = _nbr_attention(proj, cache_k, cache_v, bias_tabs[l], rope_tabs, l, dseq, n_s, t_p // dseq)
        o_r = jnp.concatenate([or_p, or_s], axis=0)
        o_a = jnp.concatenate([oa_p, oa_s], axis=0)
        x1, h2, w_t, sel_t, nb = _outproj(o_r, o_a, x, mod[l], w_out_b[l], ln_g[l], ln_b[l], wr_t[l],
                                          router_bias[l].reshape(N_EXPERTS, 1), alpha, cond_of_tile(MOE_TILE))
        used_rows, bstart, bsrc = _block_lists(nb[:, 0, :].astype(jnp.int32))
        xs = _dispatch(used_rows, h2, sel_t, w_t)
        ys = _experts(bstart, bsrc, xs, w_e_in, w_e_out, l)
        x = _combine(used_rows, ys, sel_t, h2, x1, mod[l], w_sh_in_b[l], w_sh_out_b[l], ln_g[l], ln_b[l], alpha,
                     cond_of_tile(MOE_TILE))
        new_k.append(k_l)
        new_v.append(v_l)
        new_s.append(st_p)
    return (x[:t_p].reshape(n_p, seq, d), x[t_p:].reshape(n_s, dseq, d),
            jnp.stack(new_k, axis=1), jnp.stack(new_v, axis=1), jnp.stack(new_s, axis=1))
```

```python
import functools

import jax
import jax.numpy as jnp
from jax import lax
from jax.experimental import pallas as pl
from jax.experimental.pallas import tpu as pltpu

F32 = jnp.float32
BF16 = jnp.bfloat16

D_MODEL = 2048
HEAD_DIM = 128
N_HEADS = D_MODEL // HEAD_DIM
H_R = N_HEADS // 2
H_A = N_HEADS - H_R
W_R = H_R * HEAD_DIM
W_A = H_A * HEAD_DIM
N_COLS = 5 * W_R + 3 * W_A
N_COLBLK = N_COLS // HEAD_DIM
CHUNK = 64
SUB = 16
GRID_W = 64
WIN_R = 8
WIN_C = 16
ROPE_THETA = 10000.0
N_EXPERTS = 64
TOP_K = 8
N_GROUPS = 8
GROUP_SIZE = N_EXPERTS // N_GROUPS
TOPK_GROUPS = 4
D_EXPERT = 512
D_SHARED = 512
ROUTED_SCALE = 2.5
EPS = 1e-6
LB_MAX = 1.0 - 1e-4
NEG_BIG = -1e30
EXP_CLAMP = 60.0

MOE_TILE = 256
ROW_BLK = 16
TILE_ROWS = MOE_TILE * TOP_K + N_EXPERTS * ROW_BLK
TILE_BLKS = TILE_ROWS // ROW_BLK
W_LANES = 256
XS_COLS = D_MODEL + W_LANES
DISP_ROWS = 512
COMB_SPLIT = 2
EXP_GROUP = 32
VMEM_LIMIT = 52 * 1024 * 1024


def _cparams(sem):
    return pltpu.CompilerParams(dimension_semantics=sem, vmem_limit_bytes=VMEM_LIMIT)


def _dot(a, b):
    return jnp.dot(a, b, preferred_element_type=F32)


def _dot_nt(a, b):
    return lax.dot_general(a, b, (((1,), (1,)), ((), ())), preferred_element_type=F32)


def _dot_tn(a, b):
    return lax.dot_general(a, b, (((0,), (0,)), ((), ())), preferred_element_type=F32)


def _silu(x):
    return x * jax.nn.sigmoid(x)


def _layer_norm(x):
    mu = jnp.mean(x, axis=-1, keepdims=True)
    xc = x - mu
    var = jnp.mean(xc * xc, axis=-1, keepdims=True)
    return xc * lax.rsqrt(var + EPS)


def _split3(x):
    a = x.astype(BF16)
    r = x - a.astype(F32)
    b = r.astype(BF16)
    c = (r - b.astype(F32)).astype(BF16)
    return a, b, c


def _mod_kernel(c_ref, w_ref, b_ref, o_ref):
    a = _silu(c_ref[...]).astype(BF16)
    o_ref[...] = _dot(a, w_ref[...].astype(BF16)) + b_ref[...]


def _modulation(cond, w_ada, b_ada):
    depth, d, n6 = w_ada.shape
    nc = cond.shape[0]
    tn = 1024
    return pl.pallas_call(
        _mod_kernel,
        grid=(depth, n6 // tn),
        in_specs=[
            pl.BlockSpec((nc, d), lambda l, j: (0, 0)),
            pl.BlockSpec((None, d, tn), lambda l, j: (l, 0, j)),
            pl.BlockSpec((None, 1, tn), lambda l, j: (l, 0, j)),
        ],
        out_specs=pl.BlockSpec((None, nc, tn), lambda l, j: (l, 0, j)),
        out_shape=jax.ShapeDtypeStruct((depth, nc, n6), F32),
        compiler_params=_cparams(("arbitrary", "arbitrary")),
        name="modulation",
    )(cond, w_ada, b_ada.reshape(depth, 1, n6))


def _inproj_kernel(x_ref, mod_ref, w_ref, o_ref, h_scr):
    @pl.when(pl.program_id(1) == 0)
    def _():
        y = _layer_norm(x_ref[...])
        sh = mod_ref[:, 0:D_MODEL]
        sc = mod_ref[:, D_MODEL:2 * D_MODEL]
        h_scr[...] = (y * (1.0 + sc) + sh).astype(BF16)

    res = _dot(h_scr[...], w_ref[...]).astype(BF16)
    for j in range(o_ref.shape[0]):
        o_ref[j] = res[:, j * HEAD_DIM:(j + 1) * HEAD_DIM]


def _inproj(x, mod_l, w_in_l, tm, cond_of_tile):
    t = x.shape[0]
    tn = 1024
    nblk = tn // HEAD_DIM
    nc = mod_l.shape[0]
    return pl.pallas_call(
        _inproj_kernel,
        grid=(t // tm, N_COLS // tn),
        in_specs=[
            pl.BlockSpec((tm, D_MODEL), lambda i, j: (i, 0)),
            pl.BlockSpec((None, 1, 6 * D_MODEL), lambda i, j: (cond_of_tile(i), 0, 0)),
            pl.BlockSpec((D_MODEL, tn), lambda i, j: (0, j)),
        ],
        out_specs=pl.BlockSpec((nblk, tm, HEAD_DIM), lambda i, j: (j, i, 0)),
        out_shape=jax.ShapeDtypeStruct((N_COLBLK, t, HEAD_DIM), BF16),
        scratch_shapes=[pltpu.VMEM((tm, D_MODEL), BF16)],
        compiler_params=_cparams(("arbitrary", "arbitrary")),
        name="inproj",
    )(x, mod_l.reshape(nc, 1, 6 * D_MODEL), w_in_l)


def _gates(z, log_lb, log1m_lb, one_m_lb):
    lse = jnp.log(1.0 + jnp.exp(-jnp.abs(z)))
    log_sig = jnp.minimum(z, 0.0) - lse
    k = one_m_lb * jnp.exp(log_sig - z)
    c = log1m_lb + log_sig
    g = jnp.maximum(log_lb, c) + jnp.log(1.0 + jnp.exp(-jnp.abs(log_lb - c)))
    return k, g


def _chunk_cumsum(tri, g):
    r = _dot(tri, jnp.concatenate(_split3(g), axis=1))
    return r[:, 0:HEAD_DIM] + r[:, HEAD_DIM:2 * HEAD_DIM] + r[:, 2 * HEAD_DIM:3 * HEAD_DIM]


def _intra_scores(q, k, b, reverse):
    zeros = jnp.zeros((SUB, HEAD_DIM), F32)
    nsub = CHUNK // SUB
    lhs_rows, rhs_slots = [], []
    for blk in range(nsub):
        lo = blk * SUB
        mid = lo + SUB // 2 if reverse else lo + SUB // 2 - 1
        r = b[mid:mid + 1, :]
        qe = q[lo:lo + SUB] * jnp.exp(jnp.minimum(b[lo:lo + SUB] - r, EXP_CLAMP))
        lhs_rows.append(jnp.concatenate([qe if j == blk else zeros for j in range(nsub)], axis=1))
        rhs_slots.append(k * jnp.exp(jnp.minimum(r - b, EXP_CLAMP)))
    lhs = jnp.concatenate(lhs_rows, axis=0).astype(BF16)
    rhs = jnp.concatenate(rhs_slots, axis=1).astype(BF16)
    return _dot_nt(lhs, rhs)


def _intra_scores_exact(q, k, b, reverse):
    s_idx = lax.broadcasted_iota(jnp.int32, (CHUNK, HEAD_DIM), 0)
    r_idx = lax.broadcasted_iota(jnp.int32, (SUB, HEAD_DIM), 0)
    zeros = jnp.zeros((SUB, HEAD_DIM), F32)
    out = []
    for blk in range(CHUNK // SUB):
        lo = blk * SUB
        qb = q[lo:lo + SUB]
        lhs_slots, rhs_slots = [], []
        for j in range(SUB):
            t = lo + j
            valid = (s_idx >= t) if reverse else (s_idx <= t)
            rhs_slots.append(k * jnp.exp(jnp.where(valid, b[t:t + 1, :] - b, NEG_BIG)))
            lhs_slots.append(jnp.where(r_idx == j, qb, zeros))
        lhs = jnp.concatenate(lhs_slots, axis=1).astype(BF16)
        rhs = jnp.concatenate(rhs_slots, axis=1).astype(BF16)
        out.append(_dot_nt(lhs, rhs))
    return jnp.concatenate(out, axis=0)


def _hgrn_kernel(*refs, seq_len, has_init, emit_state):
    q_ref, i_ref, zf_ref, zb_ref, g_ref, lbc_ref, nrm_ref = refs[:7]
    pos = 7
    s0_ref = None
    if has_init:
        s0_ref = refs[pos]
        pos += 1
    o_ref = refs[pos]
    pos += 1
    so_ref = None
    if emit_state:
        so_ref = refs[pos]
        pos += 1
    o_scr, qd_scr, ut_scr, dec_scr = refs[pos:pos + 4]

    n_chunks = seq_len // CHUNK
    ri = lax.broadcasted_iota(jnp.int32, (CHUNK, CHUNK), 0)
    ci = lax.broadcasted_iota(jnp.int32, (CHUNK, CHUNK), 1)
    tri_f = jnp.where(ci <= ri, 1.0, 0.0).astype(BF16)
    tri_b = jnp.where(ci >= ri, 1.0, 0.0).astype(BF16)
    lbc = lbc_ref[...]
    lbc_f = (lbc[0:1], lbc[1:2], lbc[2:3])
    lbc_b = (lbc[3:4], lbc[4:5], lbc[5:6])

    if has_init:
        st_f0 = s0_ref[0].T
        st_b0 = s0_ref[1].T
    else:
        st_f0 = jnp.zeros((HEAD_DIM, HEAD_DIM), F32)
        st_b0 = jnp.zeros((HEAD_DIM, HEAD_DIM), F32)

    per_it = 2
    dirs = ((zf_ref, lbc_f, tri_f, ci <= ri, False), (zb_ref, lbc_b, tri_b, ci >= ri, True))

    def decay_floor(z_ref, lbc_d):
        zneg = jnp.minimum(z_ref[...].astype(F32), 0.0)
        return jnp.min(jnp.maximum(lbc_d[0], lbc_d[1] + zneg - 0.6931472))

    safe = jnp.minimum(decay_floor(zf_ref, lbc_f), decay_floor(zb_ref, lbc_b)) >= -EXP_CLAMP / (SUB // 2)

    def local(it, carry, scores):
        chains = []
        for u in range(per_it):
            c = it * per_it + u
            rows = pl.ds(pl.multiple_of(c * CHUNK, CHUNK), CHUNK)
            q = _silu(q_ref[rows, :].astype(F32))
            v = i_ref[rows, :]
            for d, (z_ref, lbc_d, tri, keep, rev) in enumerate(dirs):
                k, g = _gates(z_ref[rows, :].astype(F32), lbc_d[0], lbc_d[1], lbc_d[2])
                chains.append(dict(c=c, rows=rows, d=d, q=q, v=v, k=k, g=g, tri=tri, keep=keep, rev=rev))
        for ch in chains:
            ch["b"] = _chunk_cumsum(ch["tri"], ch["g"])
        for ch in chains:
            ch["a"] = scores(ch["q"], ch["k"], ch["b"], ch["rev"])
        for ch in chains:
            d, rows, b = ch["d"], ch["rows"], ch["b"]
            a = jnp.where(ch["keep"], ch["a"], 0.0).astype(BF16)
            o_scr[d, rows, :] = _dot(a, ch["v"])
            qd_scr[d, rows, :] = (ch["q"] * jnp.exp(b)).astype(BF16)
            edge = b[0:1, :] if ch["rev"] else b[CHUNK - 1:CHUNK, :]
            ut_scr[d, ch["c"]] = _dot_tn(ch["v"], (ch["k"] * jnp.exp(edge - b)).astype(BF16))
            dec_scr[d, ch["c"]] = jnp.exp(edge)
        return carry

    @pl.when(safe)
    def _():
        lax.fori_loop(0, n_chunks // per_it, functools.partial(local, scores=_intra_scores), 0)

    @pl.when(jnp.logical_not(safe))
    def _():
        lax.fori_loop(0, n_chunks // per_it, functools.partial(local, scores=_intra_scores_exact), 0)

    def scan(j, carry):
        new = []
        for d, st in enumerate(carry):
            c = j if d == 0 else n_chunks - 1 - j
            rows = pl.ds(pl.multiple_of(c * CHUNK, CHUNK), CHUNK)
            o_scr[d, rows, :] += _dot_nt(qd_scr[d, rows, :], st.astype(BF16))
            new.append(st * dec_scr[d, c] + ut_scr[d, c])
        return tuple(new)

    st_f, st_b = lax.fori_loop(0, n_chunks, scan, (st_f0, st_b0), unroll=2)
    if emit_state:
        so_ref[0] = st_f.T
        so_ref[1] = st_b.T

    nrm = nrm_ref[...]
    piece = 256 if seq_len % 256 == 0 else CHUNK

    def fin(p, carry):
        r0 = pl.multiple_of(p * piece, piece)
        o = o_scr[0, pl.ds(r0, piece), :] + o_scr[1, pl.ds(r0, piece), :]
        o = o * lax.rsqrt(jnp.mean(o * o, axis=-1, keepdims=True) + EPS) * nrm
        gate = g_ref[pl.ds(r0, piece), :].astype(F32)
        o_ref[pl.ds(r0, piece), :] = (o * _silu(gate)).astype(BF16)
        return carry

    lax.fori_loop(0, seq_len // piece, fin, 0)


def _hgrn(proj, lbc, nrm, seq_len, n_seq, row_blk0, state_in=None, layer=0, emit_state=False):
    kern = functools.partial(_hgrn_kernel, seq_len=seq_len, has_init=state_in is not None, emit_state=emit_state)

    def col(sec):
        return pl.BlockSpec((None, seq_len, HEAD_DIM), lambda b, h: (sec * H_R + h, row_blk0 + b, 0))

    in_specs = [col(0), col(1), col(2), col(3), col(4),
                pl.BlockSpec((8, HEAD_DIM), lambda b, h: (0, h)),
                pl.BlockSpec((1, HEAD_DIM), lambda b, h: (0, 0))]
    args = [proj, proj, proj, proj, proj, lbc, nrm]
    if state_in is not None:
        in_specs.append(pl.BlockSpec((None, None, 2, None, HEAD_DIM, HEAD_DIM),
                                     lambda b, h: (b, layer, 0, h, 0, 0)))
        args.append(state_in)
    out_specs = [pl.BlockSpec((seq_len, HEAD_DIM), lambda b, h: (b, h))]
    out_shape = [jax.ShapeDtypeStruct((n_seq * seq_len, W_R), BF16)]
    if emit_state:
        out_specs.append(pl.BlockSpec((None, 2, None, HEAD_DIM, HEAD_DIM), lambda b, h: (b, 0, h, 0, 0)))
        out_shape.append(jax.ShapeDtypeStruct((n_seq, 2, H_R, HEAD_DIM, HEAD_DIM), F32))
    res = pl.pallas_call(
        kern,
        grid=(n_seq, H_R),
        in_specs=in_specs,
        out_specs=out_specs,
        out_shape=out_shape,
        scratch_shapes=[pltpu.VMEM((2, seq_len, HEAD_DIM), F32),
                        pltpu.VMEM((2, seq_len, HEAD_DIM), BF16),
                        pltpu.VMEM((2, seq_len // CHUNK, HEAD_DIM, HEAD_DIM), F32),
                        pltpu.VMEM((2, seq_len // CHUNK, 1, HEAD_DIM), F32)],
        compiler_params=_cparams(("arbitrary", "arbitrary")),
        name="hgrn_state" if emit_state else "hgrn",
    )(*args)
    return res


def _ctx_attn_kernel(q_ref, k_ref, v_ref, o_ref, ko_ref, vo_ref):
    q = q_ref[...]
    k = k_ref[...]
    v = v_ref[...]
    s = _dot_nt(q, k) * (HEAD_DIM ** -0.5)
    m = jnp.max(s, axis=-1, keepdims=True)
    p = jnp.exp(s - m)
    den = jnp.sum(p, axis=-1, keepdims=True)
    o = _dot(p.astype(BF16), v) / den
    o_ref[...] = o.astype(BF16)
    ko_ref[...] = k.astype(F32)
    vo_ref[...] = v.astype(F32)


def _ctx_attention(proj, seq_len, n_seq):
    def col(sec):
        return pl.BlockSpec((None, seq_len, HEAD_DIM), lambda b, h: (5 * H_R + sec * H_A + h, b, 0))

    cache_spec = pl.BlockSpec((None, None, seq_len, HEAD_DIM), lambda b, h: (b, h, 0, 0))
    cache_shape = jax.ShapeDtypeStruct((n_seq, H_A, seq_len, HEAD_DIM), F32)
    return pl.pallas_call(
        _ctx_attn_kernel,
        grid=(n_seq, H_A),
        in_specs=[col(0), col(1), col(2)],
        out_specs=[pl.BlockSpec((seq_len, HEAD_DIM), lambda b, h: (b, h)), cache_spec, cache_spec],
        out_shape=[jax.ShapeDtypeStruct((n_seq * seq_len, W_A), BF16), cache_shape, cache_shape],
        compiler_params=_cparams(("arbitrary", "arbitrary")),
        name="ctx_attention",
    )(proj, proj, proj)


def _rope(x, cos, sin_lo, sin_hi):
    return (x * cos + pltpu.roll(x, HEAD_DIM - HEAD_DIM // 4, axis=1) * sin_lo
            + pltpu.roll(x, HEAD_DIM // 4, axis=1) * sin_hi)


def _nbr_attn_kernel(q_ref, k_ref, v_ref, kc_ref, vc_ref, bias_ref, cos_ref, slo_ref, shi_ref, o_ref, kr_scr,
                     *, n_rows):
    rows_pc = 256

    def rope_k(p, carry):
        r0 = pl.multiple_of(p * rows_pc, rows_pc)
        sl = pl.ds(r0, rows_pc)
        kr_scr[sl, :] = _rope(k_ref[sl, :].astype(F32), cos_ref[sl, :], slo_ref[sl, :], shi_ref[sl, :]).astype(BF16)
        return carry

    lax.fori_loop(0, (n_rows * GRID_W) // rows_pc, rope_k, 0)
    kc = kc_ref[...].astype(BF16)
    vc = vc_ref[...].astype(BF16)
    scale = HEAD_DIM ** -0.5
    win = WIN_R * GRID_W

    per_it = 4 if n_rows % 4 == 0 else 1

    def rows_step(it, carry):
        items = []
        for u in range(per_it):
            r = it * per_it + u
            rs = jnp.clip(r - WIN_R // 2, 0, n_rows - WIN_R)
            qs = pl.ds(pl.multiple_of(r * GRID_W, GRID_W), GRID_W)
            ks = pl.ds(pl.multiple_of(rs * GRID_W, GRID_W), win)
            q = _rope(q_ref[qs, :].astype(F32), cos_ref[qs, :], slo_ref[qs, :], shi_ref[qs, :])
            items.append(dict(qs=qs, ks=ks, dl=rs - r + WIN_R - 1, q=(q * scale).astype(BF16)))
        for x in items:
            x["s_loc"] = _dot_nt(x["q"], kr_scr[x["ks"], :]) + bias_ref[x["dl"]]
            x["s_ctx"] = _dot_nt(x["q"], kc)
        for x in items:
            m = jnp.maximum(jnp.max(x["s_loc"], axis=-1, keepdims=True), jnp.max(x["s_ctx"], axis=-1, keepdims=True))
            p_loc = jnp.exp(x["s_loc"] - m)
            p_ctx = jnp.exp(x["s_ctx"] - m)
            x["den"] = jnp.sum(p_loc, axis=-1, keepdims=True) + jnp.sum(p_ctx, axis=-1, keepdims=True)
            x["p_loc"] = p_loc.astype(BF16)
            x["p_ctx"] = p_ctx.astype(BF16)
        for x in items:
            o = _dot(x["p_loc"], v_ref[x["ks"], :]) + _dot(x["p_ctx"], vc)
            o_ref[x["qs"], :] = (o / x["den"]).astype(BF16)
        return carry

    lax.fori_loop(0, n_rows // per_it, rows_step, 0)


def _nbr_attention(proj, cache_k, cache_v, bias_tab, rope_tabs, layer, seq_len, n_seq, row_blk0):
    n_rows = seq_len // GRID_W
    past = cache_k.shape[3]
    kern = functools.partial(_nbr_attn_kernel, n_rows=n_rows)

    def col(sec):
        return pl.BlockSpec((None, seq_len, HEAD_DIM), lambda b, h: (5 * H_R + sec * H_A + h, row_blk0 + b, 0))

    cache_spec = pl.BlockSpec((None, None, None, past, HEAD_DIM), lambda b, h: (b, layer, h, 0, 0))
    tab_spec = pl.BlockSpec((seq_len, HEAD_DIM), lambda b, h: (0, 0))
    return pl.pallas_call(
        kern,
        grid=(n_seq, H_A),
        in_specs=[col(0), col(1), col(2), cache_spec, cache_spec,
                  pl.BlockSpec((None, WIN_R, GRID_W, WIN_R * GRID_W), lambda b, h: (h, 0, 0, 0)),
                  tab_spec, tab_spec, tab_spec],
        out_specs=pl.BlockSpec((seq_len, HEAD_DIM), lambda b, h: (b, h)),
        out_shape=jax.ShapeDtypeStruct((n_seq * seq_len, W_A), BF16),
        scratch_shapes=[pltpu.VMEM((seq_len, HEAD_DIM), BF16)],
        compiler_params=_cparams(("arbitrary", "arbitrary")),
        name="nbr_attention",
    )(proj, proj, proj, cache_k, cache_v, bias_tab, *rope_tabs)


def _first_argmax_rows(vals, ids, sentinel):
    m = vals[0]
    for v in vals[1:]:
        m = jnp.maximum(m, v)
    m = jnp.max(m, axis=0, keepdims=True)
    best = None
    for v, i in zip(vals, ids):
        cand = jnp.min(jnp.where(v == m, i, sentinel), axis=0, keepdims=True)
        best = cand if best is None else jnp.minimum(best, cand)
    return m, best


def _route(s, sb):
    n = s.shape[1]
    iota8 = lax.broadcasted_iota(jnp.int32, (GROUP_SIZE, n), 0)
    neg_inf = -jnp.inf
    groups = [sb[g * GROUP_SIZE:(g + 1) * GROUP_SIZE, :] for g in range(N_GROUPS)]
    gscore = jnp.zeros((N_GROUPS, n), F32)
    for g, xg in enumerate(groups):
        m1, i1 = _first_argmax_rows([xg], [iota8], GROUP_SIZE)
        m2 = jnp.max(jnp.where(iota8 == i1, neg_inf, xg), axis=0, keepdims=True)
        gscore = jnp.where(iota8 == g, m1 + m2, gscore)
    gsel = jnp.zeros((N_GROUPS, n), F32)
    cur = gscore
    for _ in range(TOPK_GROUPS):
        _, i = _first_argmax_rows([cur], [iota8], N_GROUPS)
        hit = iota8 == i
        gsel = jnp.where(hit, 1.0, gsel)
        cur = jnp.where(hit, neg_inf, cur)
    cur = [jnp.where(gsel[g:g + 1, :] > 0.5, xg, neg_inf) for g, xg in enumerate(groups)]
    ids = [iota8 + g * GROUP_SIZE for g in range(N_GROUPS)]
    sel = [jnp.zeros((GROUP_SIZE, n), jnp.bool_) for _ in range(N_GROUPS)]
    for _ in range(TOP_K):
        _, i = _first_argmax_rows(cur, ids, N_EXPERTS)
        hits = [idg == i for idg in ids]
        sel = [jnp.logical_or(a, h) for a, h in zip(sel, hits)]
        cur = [jnp.where(h, neg_inf, c) for c, h in zip(cur, hits)]
    return sel


def _outproj_kernel(orp_ref, ors_ref, oap_ref, oas_ref, x_ref, mod_ref, w_ref, lng_ref, lnb_ref, wr_ref, rb_ref,
                    x1_ref, h2_ref, wt_ref, sel_ref, nb_ref, *, alpha, n_prompt_tiles):
    from_prompt = pl.program_id(0) < n_prompt_tiles
    o_r = jnp.where(from_prompt, orp_ref[...], ors_ref[...])
    o_a = jnp.where(from_prompt, oap_ref[...], oas_ref[...])
    mix = _dot(o_r, w_ref[0:W_R, :]) + _dot(o_a, w_ref[W_R:W_R + W_A, :])
    g1 = mod_ref[:, 2 * D_MODEL:3 * D_MODEL]
    x1 = _layer_norm(alpha * x_ref[...] + g1 * mix) * lng_ref[0:1, :] + lnb_ref[0:1, :]
    x1_ref[...] = x1
    sh2 = mod_ref[:, 3 * D_MODEL:4 * D_MODEL]
    sc2 = mod_ref[:, 4 * D_MODEL:5 * D_MODEL]
    h2 = (_layer_norm(x1) * (1.0 + sc2) + sh2).astype(BF16)
    h2_ref[...] = h2
    s = jax.nn.sigmoid(_dot_nt(wr_ref[...], h2))
    sel = _route(s, s + rb_ref[...])
    n = s.shape[1]
    wsel = [jnp.where(sel[g], s[g * GROUP_SIZE:(g + 1) * GROUP_SIZE, :], 0.0) for g in range(N_GROUPS)]
    tot = wsel[0]
    for w in wsel[1:]:
        tot = tot + w
    tot = jnp.sum(tot, axis=0, keepdims=True)
    for g in range(N_GROUPS):
        sl = slice(g * GROUP_SIZE, (g + 1) * GROUP_SIZE)
        wt_ref[sl, :] = wsel[g] / tot * ROUTED_SCALE
        sel_ref[sl, :] = jnp.where(sel[g], 1.0, 0.0)
    selv = sel_ref[...].astype(BF16)
    cnt = _dot_nt(jnp.ones((8, n), BF16), selv)
    nb_ref[...] = jnp.floor((cnt + (ROW_BLK - 1)) * (1.0 / ROW_BLK))


def _outproj(or_p, or_s, oa_p, oa_s, x, mod_l, w_out_l, ln_g_l, ln_b_l, wr_t, rbias, alpha, cond_of_tile):
    t = x.shape[0]
    tm = MOE_TILE
    nt = t // tm
    n_p = or_p.shape[0] // tm
    nc = mod_l.shape[0]
    kern = functools.partial(_outproj_kernel, alpha=alpha, n_prompt_tiles=n_p)
    row = lambda i: (i, 0)
    const = lambda i: (0, 0)
    prompt_row = lambda i: (jnp.minimum(i, n_p - 1), 0)
    sample_row = lambda i: (jnp.maximum(i - n_p, 0), 0)
    return pl.pallas_call(
        kern,
        grid=(nt,),
        in_specs=[
            pl.BlockSpec((tm, W_R), prompt_row),
            pl.BlockSpec((tm, W_R), sample_row),
            pl.BlockSpec((tm, W_A), prompt_row),
            pl.BlockSpec((tm, W_A), sample_row),
            pl.BlockSpec((tm, D_MODEL), row),
            pl.BlockSpec((None, 1, 6 * D_MODEL), lambda i: (cond_of_tile(i), 0, 0)),
            pl.BlockSpec((W_R + W_A, D_MODEL), const),
            pl.BlockSpec((2, D_MODEL), const),
            pl.BlockSpec((2, D_MODEL), const),
            pl.BlockSpec((N_EXPERTS, D_MODEL), const),
            pl.BlockSpec((N_EXPERTS, 1), const),
        ],
        out_specs=[
            pl.BlockSpec((tm, D_MODEL), row),
            pl.BlockSpec((tm, D_MODEL), row),
            pl.BlockSpec((N_EXPERTS, tm), lambda i: (0, i)),
            pl.BlockSpec((N_EXPERTS, tm), lambda i: (0, i)),
            pl.BlockSpec((None, 8, N_EXPERTS), lambda i: (i, 0, 0)),
        ],
        out_shape=[
            jax.ShapeDtypeStruct((t, D_MODEL), F32),
            jax.ShapeDtypeStruct((t, D_MODEL), BF16),
            jax.ShapeDtypeStruct((N_EXPERTS, t), F32),
            jax.ShapeDtypeStruct((N_EXPERTS, t), F32),
            jax.ShapeDtypeStruct((nt, 8, N_EXPERTS), F32),
        ],
        compiler_params=_cparams(("arbitrary",)),
        name="outproj_router",
    )(or_p, or_s, oa_p, oa_s, x, mod_l.reshape(nc, 1, 6 * D_MODEL), w_out_l, ln_g_l, ln_b_l, wr_t, rbias)


def _sorted_layout(sel):
    n = sel.shape[1]
    selb = sel.astype(BF16)
    ti = lax.broadcasted_iota(jnp.int32, (n, n), 0)
    tj = lax.broadcasted_iota(jnp.int32, (n, n), 1)
    rank = _dot(selb, jnp.where(ti < tj, 1.0, 0.0).astype(BF16))
    cnt = _dot(selb, jnp.ones((n, n), BF16))
    nb = jnp.floor((cnt + (ROW_BLK - 1)) * (1.0 / ROW_BLK))
    ei = lax.broadcasted_iota(jnp.int32, (N_EXPERTS, N_EXPERTS), 0)
    ej = lax.broadcasted_iota(jnp.int32, (N_EXPERTS, N_EXPERTS), 1)
    boff = _dot(jnp.where(ej < ei, 1.0, 0.0).astype(BF16), nb.astype(BF16))
    eye = lax.broadcasted_iota(jnp.int32, (N_EXPERTS, n), 0) == lax.broadcasted_iota(jnp.int32, (N_EXPERTS, n), 1)
    ones8 = jnp.ones((8, N_EXPERTS), BF16)
    boff_row = _dot(ones8, jnp.where(eye, boff, 0.0).astype(BF16))[0:1, 0:N_EXPERTS]
    nb_row = _dot(ones8, jnp.where(eye, nb, 0.0).astype(BF16))[0:1, 0:N_EXPERTS]
    rankp = jnp.where(sel > 0.5, rank + 1.0, 0.0)
    return rankp, boff, boff_row, nb_row


def _row_expert_onehot(row0, n_rows, boff_row, nb_row):
    blk = (lax.broadcasted_iota(jnp.int32, (n_rows, N_EXPERTS), 0) + row0) // ROW_BLK
    blk = blk.astype(F32)
    return jnp.where(jnp.logical_and(blk >= boff_row, blk < boff_row + nb_row), 1.0, 0.0).astype(BF16)


def _sort_onehots(chunks, keys, boff_row, nb_row):
    n = keys.shape[1] // 2
    onehots = [_row_expert_onehot(c * DISP_ROWS, DISP_ROWS, boff_row, nb_row) for c in chunks]
    looked = [_dot(e, keys) for e in onehots]
    out = []
    for c, qk in zip(chunks, looked):
        row = lax.broadcasted_iota(jnp.int32, (DISP_ROWS, n), 0) + (c * DISP_ROWS + 1)
        out.append(jnp.where(qk[:, 0:n] == row.astype(F32) - ROW_BLK * qk[:, n:2 * n], 1.0, 0.0).astype(BF16))
    return out


def _dispatch_kernel(used_ref, h2_ref, sel_ref, wt_ref, xs_ref, xa_scr):
    s = pl.program_id(0)
    n = h2_ref.shape[0]
    rankp, boff, boff_row, nb_row = _sorted_layout(sel_ref[...])
    keys = jnp.concatenate([rankp, boff], axis=1).astype(BF16)
    ti = lax.broadcasted_iota(jnp.int32, (n, n), 0)
    tj = lax.broadcasted_iota(jnp.int32, (n, n), 1)
    eye = jnp.where(ti == tj, 1.0, 0.0).astype(BF16)
    w1, w2, w3 = _split3(wt_ref[...])
    xa_scr[:, 0:D_MODEL] = h2_ref[...]
    xa_scr[:, D_MODEL:D_MODEL + 128] = _dot_nt(eye, jnp.concatenate([w1, w2], axis=0)).astype(BF16)
    xa_scr[:, D_MODEL + 128:XS_COLS] = _dot_nt(eye, jnp.concatenate([w3, jnp.zeros_like(w3)], axis=0)).astype(BF16)

    def emit(chunks):
        for c, g in zip(chunks, _sort_onehots(chunks, keys, boff_row, nb_row)):
            xs_ref[c * DISP_ROWS:(c + 1) * DISP_ROWS, :] = _dot(g, xa_scr[...]).astype(BF16)

    always = (MOE_TILE * TOP_K) // DISP_ROWS
    emit(tuple(range(always)))
    for c in range(always, TILE_ROWS // DISP_ROWS):
        @pl.when(c * DISP_ROWS < used_ref[s])
        def _():
            emit((c,))

        @pl.when(c * DISP_ROWS >= used_ref[s])
        def _():
            xs_ref[c * DISP_ROWS:(c + 1) * DISP_ROWS, :] = jnp.zeros((DISP_ROWS, XS_COLS), BF16)


def _dispatch(used_rows, h2, sel_t, w_t):
    t = h2.shape[0]
    nt = t // MOE_TILE
    return pl.pallas_call(
        _dispatch_kernel,
        grid_spec=pltpu.PrefetchScalarGridSpec(
            num_scalar_prefetch=1,
            grid=(nt,),
            in_specs=[
                pl.BlockSpec((MOE_TILE, D_MODEL), lambda s, u: (s, 0)),
                pl.BlockSpec((N_EXPERTS, MOE_TILE), lambda s, u: (0, s)),
                pl.BlockSpec((N_EXPERTS, MOE_TILE), lambda s, u: (0, s)),
            ],
            out_specs=pl.BlockSpec((TILE_ROWS, XS_COLS), lambda s, u: (s, 0)),
            scratch_shapes=[pltpu.VMEM((MOE_TILE, XS_COLS), BF16)],
        ),
        out_shape=jax.ShapeDtypeStruct((nt * TILE_ROWS, XS_COLS), BF16),
        compiler_params=_cparams(("arbitrary",)),
        name="moe_dispatch",
    )(used_rows, h2, sel_t, w_t)


def _expert_kernel(bstart_ref, bsrc_ref, xs_hbm, w1_ref, w2_ref, ys_hbm, w1_scr, w2_scr, x_buf, y_buf, sem_in, sem_out):
    e = pl.program_id(0)
    b0 = bstart_ref[e]
    n_blk = bstart_ref[e + 1] - b0
    n_groups = (n_blk + EXP_GROUP - 1) // EXP_GROUP

    def in_copy(j, slot, k):
        src = pl.multiple_of(bsrc_ref[j] * ROW_BLK, ROW_BLK)
        row = k * ROW_BLK if isinstance(k, int) else pl.multiple_of(k * ROW_BLK, ROW_BLK)
        return pltpu.make_async_copy(xs_hbm.at[pl.ds(src, ROW_BLK), :],
                                     x_buf.at[slot, pl.ds(row, ROW_BLK), :], sem_in.at[slot])

    def out_copy(j, slot, k):
        dst = pl.multiple_of(bsrc_ref[j] * ROW_BLK, ROW_BLK)
        row = k * ROW_BLK if isinstance(k, int) else pl.multiple_of(k * ROW_BLK, ROW_BLK)
        return pltpu.make_async_copy(y_buf.at[slot, pl.ds(row, ROW_BLK), :],
                                     ys_hbm.at[pl.ds(dst, ROW_BLK), pl.ds(0, D_MODEL)], sem_out.at[slot])

    def for_blocks(gi, fn):
        k0 = gi * EXP_GROUP
        cnt = jnp.minimum(n_blk - k0, EXP_GROUP)

        @pl.when(cnt == EXP_GROUP)
        def _():
            for k in range(EXP_GROUP):
                fn(b0 + k0 + k, k)

        @pl.when(cnt < EXP_GROUP)
        def _():
            def body(k, carry):
                fn(b0 + k0 + k, k)
                return carry

            lax.fori_loop(0, cnt, body, 0)

    @pl.when(e == 0)
    def _():
        x_buf[...] = jnp.zeros(x_buf.shape, BF16)

    @pl.when(n_groups > 0)
    def _():
        for_blocks(0, lambda j, k: in_copy(j, 0, k).start())

    w1_scr[...] = w1_ref[...].astype(BF16)
    w2_scr[...] = w2_ref[...].astype(BF16)

    def group(gi, carry):
        slot = gi % 2

        @pl.when(gi + 1 < n_groups)
        def _():
            for_blocks(gi + 1, lambda j, k: in_copy(j, 1 - slot, k).start())

        for_blocks(gi, lambda j, k: in_copy(j, slot, k).wait())

        @pl.when(gi >= 2)
        def _():
            for_blocks(gi - 2, lambda j, k: out_copy(j, slot, k).wait())

        x = x_buf[slot, :, 0:D_MODEL]
        lane = lax.broadcasted_iota(jnp.int32, (x_buf.shape[1], W_LANES), 1)
        mine = (lane & (N_EXPERTS - 1)) == e
        wrow = jnp.sum(jnp.where(mine, x_buf[slot, :, D_MODEL:XS_COLS].astype(F32), 0.0), axis=1, keepdims=True)
        h = _dot(x, w1_scr[...])
        act = (_silu(h[:, 0:D_EXPERT]) * h[:, D_EXPERT:2 * D_EXPERT]).astype(BF16)
        y_buf[slot] = (_dot(act, w2_scr[...]) * wrow).astype(BF16)
        for_blocks(gi, lambda j, k: out_copy(j, slot, k).start())
        return carry

    lax.fori_loop(0, n_groups, group, 0)

    for back in (2, 1):
        @pl.when(n_groups >= back)
        def _():
            g_last = n_groups - back
            for_blocks(g_last, lambda j, k: out_copy(j, g_last % 2, k).wait())


def _experts(bstart, bsrc, xs, w_e_in, w_e_out, layer):
    rows = EXP_GROUP * ROW_BLK
    return pl.pallas_call(
        _expert_kernel,
        grid_spec=pltpu.PrefetchScalarGridSpec(
            num_scalar_prefetch=2,
            grid=(N_EXPERTS,),
            in_specs=[
                pl.BlockSpec(memory_space=pl.ANY),
                pl.BlockSpec((None, None, D_MODEL, 2 * D_EXPERT), lambda e, bs, br: (layer, e, 0, 0)),
                pl.BlockSpec((None, None, D_EXPERT, D_MODEL), lambda e, bs, br: (layer, e, 0, 0)),
            ],
            out_specs=pl.BlockSpec(memory_space=pl.ANY),
            scratch_shapes=[
                pltpu.VMEM((D_MODEL, 2 * D_EXPERT), BF16),
                pltpu.VMEM((D_EXPERT, D_MODEL), BF16),
                pltpu.VMEM((2, rows, XS_COLS), BF16),
                pltpu.VMEM((2, rows, D_MODEL), BF16),
                pltpu.SemaphoreType.DMA((2,)),
                pltpu.SemaphoreType.DMA((2,)),
            ],
        ),
        out_shape=jax.ShapeDtypeStruct(xs.shape, xs.dtype),
        input_output_aliases={2: 0},
        compiler_params=_cparams(("arbitrary",)),
        name="moe_experts",
    )(bstart, bsrc, xs, w_e_in, w_e_out)


def _combine_kernel(used_ref, ys_ref, sel_ref, h2_ref, x1_ref, mod_ref, wsi_ref, wso_ref, lng_ref, lnb_ref,
                    o_ref, acc_scr, keyt_scr, row_scr, *, alpha):
    s = pl.program_id(0)
    half = pl.program_id(1)
    n = h2_ref.shape[0]
    per_step = TILE_ROWS // DISP_ROWS // COMB_SPLIT
    always = (MOE_TILE * TOP_K) // DISP_ROWS

    @pl.when(half == 0)
    def _():
        rankp, _, boff_row, nb_row = _sorted_layout(sel_ref[...])
        ti = lax.broadcasted_iota(jnp.int32, (n, n), 0)
        tj = lax.broadcasted_iota(jnp.int32, (n, n), 1)
        eye = jnp.where(ti == tj, 1.0, 0.0).astype(BF16)
        keyt_scr[0:n, :] = _dot_nt(eye, rankp.astype(BF16)).astype(BF16)
        keyt_scr[n:2 * n, :] = jnp.broadcast_to(boff_row, (n, N_EXPERTS)).astype(BF16)
        row_scr[0:1, :] = boff_row
        row_scr[1:2, :] = nb_row
        h = _dot(h2_ref[...], wsi_ref[...])
        act = (_silu(h[:, 0:D_SHARED]) * h[:, D_SHARED:2 * D_SHARED]).astype(BF16)
        acc_scr[...] = _dot(act, wso_ref[...])

    def unsort(chunks):
        onehots = [_row_expert_onehot(g * DISP_ROWS, DISP_ROWS, row_scr[0:1, :], row_scr[1:2, :]) for g in chunks]
        looked = [_dot_nt(keyt_scr[...], e) for e in onehots]
        total = None
        for g, qk in zip(chunks, looked):
            row = lax.broadcasted_iota(jnp.int32, (n, DISP_ROWS), 1) + (g * DISP_ROWS + 1)
            c = jnp.where(qk[0:n] == row.astype(F32) - ROW_BLK * qk[n:2 * n], 1.0, 0.0).astype(BF16)
            lo = (g % per_step) * DISP_ROWS
            part = _dot(c, ys_ref[lo:lo + DISP_ROWS, :])
            total = part if total is None else total + part
        acc_scr[...] += total

    for step in range(COMB_SPLIT):
        @pl.when(half == step)
        def _():
            mine = range(step * per_step, (step + 1) * per_step)
            sure = tuple(g for g in mine if g < always)
            if sure:
                unsort(sure)
            for g in mine:
                if g >= always:
                    @pl.when(g * DISP_ROWS < used_ref[s])
                    def _():
                        unsort((g,))

    @pl.when(half == COMB_SPLIT - 1)
    def _():
        g2 = mod_ref[:, 5 * D_MODEL:6 * D_MODEL]
        y = _layer_norm(alpha * x1_ref[...] + g2 * acc_scr[...])
        o_ref[...] = y * lng_ref[1:2, :] + lnb_ref[1:2, :]


def _combine(used_rows, ys, sel_t, h2, x1, mod_l, w_sh_in_l, w_sh_out_l, ln_g_l, ln_b_l, alpha, cond_of_tile):
    t = h2.shape[0]
    nt = t // MOE_TILE
    nrc = COMB_SPLIT
    comb_rows = TILE_ROWS // COMB_SPLIT
    nc = mod_l.shape[0]
    kern = functools.partial(_combine_kernel, alpha=alpha)
    row = lambda s, r, u: (s, 0)
    const = lambda s, r, u: (0, 0)
    return pl.pallas_call(
        kern,
        grid_spec=pltpu.PrefetchScalarGridSpec(
            num_scalar_prefetch=1,
            grid=(nt, nrc),
            in_specs=[
                pl.BlockSpec((comb_rows, D_MODEL), lambda s, r, u: (s * nrc + r, 0)),
                pl.BlockSpec((N_EXPERTS, MOE_TILE), lambda s, r, u: (0, s)),
                pl.BlockSpec((MOE_TILE, D_MODEL), row),
                pl.BlockSpec((MOE_TILE, D_MODEL), row),
                pl.BlockSpec((None, 1, 6 * D_MODEL), lambda s, r, u: (cond_of_tile(s), 0, 0)),
                pl.BlockSpec((D_MODEL, 2 * D_SHARED), const),
                pl.BlockSpec((D_SHARED, D_MODEL), const),
                pl.BlockSpec((2, D_MODEL), const),
                pl.BlockSpec((2, D_MODEL), const),
            ],
            out_specs=pl.BlockSpec((MOE_TILE, D_MODEL), row),
            scratch_shapes=[
                pltpu.VMEM((MOE_TILE, D_MODEL), F32),
                pltpu.VMEM((2 * MOE_TILE, N_EXPERTS), BF16),
                pltpu.VMEM((8, N_EXPERTS), F32),
            ],
        ),
        out_shape=jax.ShapeDtypeStruct((t, D_MODEL), F32),
        compiler_params=_cparams(("arbitrary", "arbitrary")),
        name="moe_combine",
    )(used_rows, ys, sel_t, h2, x1, mod_l.reshape(nc, 1, 6 * D_MODEL), w_sh_in_l, w_sh_out_l, ln_g_l, ln_b_l)


def _block_lists(nb):
    nt = nb.shape[0]
    cum = jnp.cumsum(nb, axis=1)
    boff = cum - nb
    used_blk = cum[:, -1]
    per_e = jnp.sum(nb, axis=0)
    bstart = jnp.concatenate([jnp.zeros((1,), jnp.int32), jnp.cumsum(per_e).astype(jnp.int32)])
    pref = jnp.cumsum(nb, axis=0) - nb
    lb = jnp.arange(TILE_BLKS, dtype=jnp.int32)
    owner = (lb[None, :, None] >= boff[:, None, :]) & (lb[None, :, None] < cum[:, None, :])
    base = bstart[None, :-1] + pref - boff
    pos = jnp.sum(jnp.where(owner, base[:, None, :], 0), axis=-1) + lb[None, :]
    total = nt * TILE_BLKS
    pos = jnp.where(jnp.any(owner, axis=-1), pos, total)
    src = (jnp.arange(nt, dtype=jnp.int32)[:, None] * TILE_BLKS + lb[None, :]).reshape(-1)
    bsrc = jnp.zeros((total,), jnp.int32).at[pos.reshape(-1)].set(src, mode="drop")
    return (used_blk * ROW_BLK).astype(jnp.int32), bstart, bsrc


def _lower_bound_consts(lb_raw):
    p = jax.nn.softmax(lb_raw.astype(F32), axis=0)
    cs = jnp.cumsum(p, axis=0)
    lb = jnp.clip(jnp.concatenate([jnp.zeros_like(cs[:1]), cs[:-1]], axis=0), 0.0, LB_MAX)
    rows = jnp.stack([jnp.log(lb[:, 0]), jnp.log1p(-lb[:, 0]), 1.0 - lb[:, 0],
                      jnp.log(lb[:, 1]), jnp.log1p(-lb[:, 1]), 1.0 - lb[:, 1]], axis=1)
    return jnp.concatenate([rows, jnp.zeros((rows.shape[0], 2, W_R), F32)], axis=1)


def _rope_tables(seq_len):
    t = jnp.arange(seq_len)
    half = HEAD_DIM // 2
    nf = half // 2
    inv = ROPE_THETA ** (-jnp.arange(nf, dtype=F32) / nf)

    def tabs(pos):
        ang = pos[:, None].astype(F32) * inv[None, :]
        cos = jnp.concatenate([jnp.cos(ang), jnp.cos(ang)], -1)
        sin = jnp.sin(ang)
        zero = jnp.zeros_like(sin)
        return cos, jnp.concatenate([-sin, zero], -1), jnp.concatenate([zero, sin], -1)

    row = tabs(t // GRID_W)
    colt = tabs(t % GRID_W)
    return tuple(jnp.concatenate([a, b], -1) for a, b in zip(row, colt))


def _bias_tables(rpb):
    cq = jnp.arange(GRID_W)
    cs = jnp.clip(cq - WIN_C // 2, 0, GRID_W - WIN_C)
    valid = (cq[None, :] >= cs[:, None]) & (cq[None, :] < cs[:, None] + WIN_C)
    coff = jnp.clip(cq[None, :] - cq[:, None], -(WIN_C - 1), WIN_C - 1) + WIN_C - 1
    roff = jnp.arange(WIN_R)[:, None] + jnp.arange(WIN_R)[None, :]
    pick_c = (coff[None] == jnp.arange(2 * WIN_C - 1)[:, None, None]).astype(F32)
    pick_r = (roff[None] == jnp.arange(2 * WIN_R - 1)[:, None, None]).astype(F32)
    hi = lax.Precision.HIGHEST
    bias_c = jnp.einsum('dhrc,cqw->dhrqw', rpb.astype(F32), pick_c, precision=hi)
    tab = jnp.einsum('rlk,dhrqw->dhlqkw', pick_r, bias_c, precision=hi)
    tab = jnp.where(valid[None, None, None, :, None, :], tab, NEG_BIG)
    return tab.reshape(tab.shape[0], H_A, WIN_R, GRID_W, WIN_R * GRID_W)


def kernel(x_prompt, x_sample, cache_k, cache_v, state_hgrn, c, c_ctx, w_ada, b_ada, w_in, w_out, lb_raw, hgrn_norm,
           rpb, ln_g, ln_b, w_router, router_bias, w_e_in, w_e_out, w_sh_in, w_sh_out):
    n_p, seq, d = x_prompt.shape
    n_s, dseq, _ = x_sample.shape
    depth = w_in.shape[0]
    t_p, t_s = n_p * seq, n_s * dseq
    t = t_p + t_s
    alpha = (2 * depth) ** 0.25
    assert d == D_MODEL and seq % MOE_TILE == 0 and dseq % MOE_TILE == 0 and t_p % dseq == 0
    assert dseq % GRID_W == 0 and dseq // GRID_W >= WIN_R

    tm_in = next(m for m in (1024, 512, 256) if t_p % m == 0 and dseq % m == 0)

    def cond_of_tile(tile_rows):
        n_p_tiles = t_p // tile_rows
        per_seq = dseq // tile_rows
        return lambda i: jnp.where(i < n_p_tiles, 0, 1 + (i - n_p_tiles) // per_seq)

    n_cond = -(-(1 + n_s) // 8) * 8
    cond = jnp.zeros((n_cond, d), F32).at[0].set(c_ctx).at[1:1 + n_s].set(c)
    mod = _modulation(cond, w_ada, b_ada)

    lbc = _lower_bound_consts(lb_raw)
    rope_tabs = _rope_tables(dseq)
    bias_tabs = _bias_tables(rpb)
    w_in_b = w_in.astype(BF16)
    w_out_b = w_out.astype(BF16)
    w_sh_in_b = w_sh_in.astype(BF16)
    w_sh_out_b = w_sh_out.astype(BF16)
    wr_t = jnp.swapaxes(w_router, 1, 2).astype(BF16)

    x = jnp.concatenate([x_prompt.reshape(t_p, d), x_sample.reshape(t_s, d)], axis=0)
    new_k, new_v, new_s = [], [], []
    for l in range(depth):
        proj = _inproj(x, mod[l], w_in_b[l], tm_in, cond_of_tile(tm_in))
        nrm = hgrn_norm[l].reshape(1, HEAD_DIM)
        or_p, st_p = _hgrn(proj, lbc[l], nrm, seq, n_p, 0, emit_state=True)
        (or_s,) = _hgrn(proj, lbc[l], nrm, dseq, n_s, t_p // dseq, state_in=state_hgrn, layer=l)
        oa_p, k_l, v_l = _ctx_attention(proj, seq, n_p)
        oa_s = _nbr_attention(proj, cache_k, cache_v, bias_tabs[l], rope_tabs, l, dseq, n_s, t_p // dseq)
        x1, h2, w_t, sel_t, nb = _outproj(or_p, or_s, oa_p, oa_s, x, mod[l], w_out_b[l], ln_g[l], ln_b[l], wr_t[l],
                                          router_bias[l].reshape(N_EXPERTS, 1), alpha, cond_of_tile(MOE_TILE))
        used_rows, bstart, bsrc = _block_lists(nb[:, 0, :].astype(jnp.int32))
        xs = _dispatch(used_rows, h2, sel_t, w_t)
        ys = _experts(bstart, bsrc, xs, w_e_in, w_e_out, l)
        x = _combine(used_rows, ys, sel_t, h2, x1, mod[l], w_sh_in_b[l], w_sh_out_b[l], ln_g[l], ln_b[l], alpha,
                     cond_of_tile(MOE_TILE))
        new_k.append(k_l)
        new_v.append(v_l)
        new_s.append(st_p)
    return (x[:t_p].reshape(n_p, seq, d), x[t_p:].reshape(n_s, dseq, d),
            jnp.stack(new_k, axis=1), jnp.stack(new_v, axis=1), jnp.stack(new_s, axis=1))
```

```python
import functools

import jax
import jax.numpy as jnp
from jax import lax
from jax.experimental import pallas as pl
from jax.experimental.pallas import tpu as pltpu

F32 = jnp.float32
BF16 = jnp.bfloat16

D_MODEL = 2048
HEAD_DIM = 128
N_HEADS = D_MODEL // HEAD_DIM
H_R = N_HEADS // 2
H_A = N_HEADS - H_R
W_R = H_R * HEAD_DIM
W_A = H_A * HEAD_DIM
N_COLS = 5 * W_R + 3 * W_A
N_COLBLK = N_COLS // HEAD_DIM
CHUNK = 64
SUB = 16
GRID_W = 64
WIN_R = 8
WIN_C = 16
ROPE_THETA = 10000.0
N_EXPERTS = 64
TOP_K = 8
N_GROUPS = 8
GROUP_SIZE = N_EXPERTS // N_GROUPS
TOPK_GROUPS = 4
D_EXPERT = 512
D_SHARED = 512
ROUTED_SCALE = 2.5
EPS = 1e-6
LB_MAX = 1.0 - 1e-4
NEG_BIG = -1e30
EXP_CLAMP = 60.0

MOE_TILE = 256
ROW_BLK = 16
TILE_ROWS = MOE_TILE * TOP_K + N_EXPERTS * ROW_BLK
TILE_BLKS = TILE_ROWS // ROW_BLK
W_LANES = 256
XS_COLS = D_MODEL + W_LANES
DISP_ROWS = 512
COMB_SPLIT = 2
EXP_GROUP = 32
VMEM_LIMIT = 52 * 1024 * 1024


def _cparams(sem):
    return pltpu.CompilerParams(dimension_semantics=sem, vmem_limit_bytes=VMEM_LIMIT)


def _dot(a, b):
    return jnp.dot(a, b, preferred_element_type=F32)


def _dot_nt(a, b):
    return lax.dot_general(a, b, (((1,), (1,)), ((), ())), preferred_element_type=F32)


def _dot_tn(a, b):
    return lax.dot_general(a, b, (((0,), (0,)), ((), ())), preferred_element_type=F32)


def _silu(x):
    return x * jax.nn.sigmoid(x)


def _layer_norm(x):
    mu = jnp.mean(x, axis=-1, keepdims=True)
    xc = x - mu
    var = jnp.mean(xc * xc, axis=-1, keepdims=True)
    return xc * lax.rsqrt(var + EPS)


def _split3(x):
    a = x.astype(BF16)
    r = x - a.astype(F32)
    b = r.astype(BF16)
    c = (r - b.astype(F32)).astype(BF16)
    return a, b, c


def _mod_kernel(c_ref, w_ref, b_ref, o_ref):
    a = _silu(c_ref[...]).astype(BF16)
    o_ref[...] = _dot(a, w_ref[...].astype(BF16)) + b_ref[...]


def _modulation(cond, w_ada, b_ada):
    depth, d, n6 = w_ada.shape
    nc = cond.shape[0]
    tn = 1024
    return pl.pallas_call(
        _mod_kernel,
        grid=(depth, n6 // tn),
        in_specs=[
            pl.BlockSpec((nc, d), lambda l, j: (0, 0)),
            pl.BlockSpec((None, d, tn), lambda l, j: (l, 0, j)),
            pl.BlockSpec((None, 1, tn), lambda l, j: (l, 0, j)),
        ],
        out_specs=pl.BlockSpec((None, nc, tn), lambda l, j: (l, 0, j)),
        out_shape=jax.ShapeDtypeStruct((depth, nc, n6), F32),
        compiler_params=_cparams(("arbitrary", "arbitrary")),
        name="modulation",
    )(cond, w_ada, b_ada.reshape(depth, 1, n6))


def _inproj_kernel(x_ref, mod_ref, w_ref, o_ref, h_scr):
    @pl.when(pl.program_id(1) == 0)
    def _():
        y = _layer_norm(x_ref[...])
        sh = mod_ref[:, 0:D_MODEL]
        sc = mod_ref[:, D_MODEL:2 * D_MODEL]
        h_scr[...] = (y * (1.0 + sc) + sh).astype(BF16)

    res = _dot(h_scr[...], w_ref[...]).astype(BF16)
    for j in range(o_ref.shape[0]):
        o_ref[j] = res[:, j * HEAD_DIM:(j + 1) * HEAD_DIM]


def _inproj(x, mod_l, w_in_l, tm, cond_of_tile):
    t = x.shape[0]
    tn = 1024
    nblk = tn // HEAD_DIM
    nc = mod_l.shape[0]
    return pl.pallas_call(
        _inproj_kernel,
        grid=(t // tm, N_COLS // tn),
        in_specs=[
            pl.BlockSpec((tm, D_MODEL), lambda i, j: (i, 0)),
            pl.BlockSpec((None, 1, 6 * D_MODEL), lambda i, j: (cond_of_tile(i), 0, 0)),
            pl.BlockSpec((D_MODEL, tn), lambda i, j: (0, j)),
        ],
        out_specs=pl.BlockSpec((nblk, tm, HEAD_DIM), lambda i, j: (j, i, 0)),
        out_shape=jax.ShapeDtypeStruct((N_COLBLK, t, HEAD_DIM), BF16),
        scratch_shapes=[pltpu.VMEM((tm, D_MODEL), BF16)],
        compiler_params=_cparams(("arbitrary", "arbitrary")),
        name="inproj",
    )(x, mod_l.reshape(nc, 1, 6 * D_MODEL), w_in_l)


def _gates(z, log_lb, log1m_lb, one_m_lb):
    lse = jnp.log(1.0 + jnp.exp(-jnp.abs(z)))
    log_sig = jnp.minimum(z, 0.0) - lse
    k = one_m_lb * jnp.exp(log_sig - z)
    c = log1m_lb + log_sig
    g = jnp.maximum(log_lb, c) + jnp.log(1.0 + jnp.exp(-jnp.abs(log_lb - c)))
    return k, g


def _chunk_cumsum(tri, g):
    r = _dot(tri, jnp.concatenate(_split3(g), axis=1))
    return r[:, 0:HEAD_DIM] + r[:, HEAD_DIM:2 * HEAD_DIM] + r[:, 2 * HEAD_DIM:3 * HEAD_DIM]


def _intra_scores(q, k, b, reverse):
    zeros = jnp.zeros((SUB, HEAD_DIM), F32)
    nsub = CHUNK // SUB
    lhs_rows, rhs_slots = [], []
    for blk in range(nsub):
        lo = blk * SUB
        mid = lo + SUB // 2 if reverse else lo + SUB // 2 - 1
        r = b[mid:mid + 1, :]
        qe = q[lo:lo + SUB] * jnp.exp(jnp.minimum(b[lo:lo + SUB] - r, EXP_CLAMP))
        lhs_rows.append(jnp.concatenate([qe if j == blk else zeros for j in range(nsub)], axis=1))
        rhs_slots.append(k * jnp.exp(jnp.minimum(r - b, EXP_CLAMP)))
    lhs = jnp.concatenate(lhs_rows, axis=0).astype(BF16)
    rhs = jnp.concatenate(rhs_slots, axis=1).astype(BF16)
    return _dot_nt(lhs, rhs)


def _intra_scores_exact(q, k, b, reverse):
    s_idx = lax.broadcasted_iota(jnp.int32, (CHUNK, HEAD_DIM), 0)
    r_idx = lax.broadcasted_iota(jnp.int32, (SUB, HEAD_DIM), 0)
    zeros = jnp.zeros((SUB, HEAD_DIM), F32)
    out = []
    for blk in range(CHUNK // SUB):
        lo = blk * SUB
        qb = q[lo:lo + SUB]
        lhs_slots, rhs_slots = [], []
        for j in range(SUB):
            t = lo + j
            valid = (s_idx >= t) if reverse else (s_idx <= t)
            rhs_slots.append(k * jnp.exp(jnp.where(valid, b[t:t + 1, :] - b, NEG_BIG)))
            lhs_slots.append(jnp.where(r_idx == j, qb, zeros))
        lhs = jnp.concatenate(lhs_slots, axis=1).astype(BF16)
        rhs = jnp.concatenate(rhs_slots, axis=1).astype(BF16)
        out.append(_dot_nt(lhs, rhs))
    return jnp.concatenate(out, axis=0)


def _hgrn_kernel(*refs, seq_len, has_init, emit_state):
    q_ref, i_ref, zf_ref, zb_ref, g_ref, lbc_ref, nrm_ref = refs[:7]
    pos = 7
    s0_ref = None
    if has_init:
        s0_ref = refs[pos]
        pos += 1
    o_ref = refs[pos]
    pos += 1
    so_ref = None
    if emit_state:
        so_ref = refs[pos]
        pos += 1
    o_scr, qd_scr, ut_scr, dec_scr = refs[pos:pos + 4]

    n_chunks = seq_len // CHUNK
    ri = lax.broadcasted_iota(jnp.int32, (CHUNK, CHUNK), 0)
    ci = lax.broadcasted_iota(jnp.int32, (CHUNK, CHUNK), 1)
    tri_f = jnp.where(ci <= ri, 1.0, 0.0).astype(BF16)
    tri_b = jnp.where(ci >= ri, 1.0, 0.0).astype(BF16)
    lbc = lbc_ref[...]
    lbc_f = (lbc[0:1], lbc[1:2], lbc[2:3])
    lbc_b = (lbc[3:4], lbc[4:5], lbc[5:6])

    if has_init:
        st_f0 = s0_ref[0].T
        st_b0 = s0_ref[1].T
    else:
        st_f0 = jnp.zeros((HEAD_DIM, HEAD_DIM), F32)
        st_b0 = jnp.zeros((HEAD_DIM, HEAD_DIM), F32)

    per_it = 4
    dirs = ((zf_ref, lbc_f, tri_f, ci <= ri, False), (zb_ref, lbc_b, tri_b, ci >= ri, True))

    def decay_floor(z_ref, lbc_d):
        zneg = jnp.minimum(z_ref[...].astype(F32), 0.0)
        return jnp.min(jnp.maximum(lbc_d[0], lbc_d[1] + zneg - 0.6931472))

    safe = jnp.minimum(decay_floor(zf_ref, lbc_f), decay_floor(zb_ref, lbc_b)) >= -EXP_CLAMP / (SUB // 2)

    def local(it, carry, scores):
        chains = []
        for u in range(per_it):
            c = it * per_it + u
            rows = pl.ds(pl.multiple_of(c * CHUNK, CHUNK), CHUNK)
            q = _silu(q_ref[rows, :].astype(F32))
            v = i_ref[rows, :]
            for d, (z_ref, lbc_d, tri, keep, rev) in enumerate(dirs):
                k, g = _gates(z_ref[rows, :].astype(F32), lbc_d[0], lbc_d[1], lbc_d[2])
                chains.append(dict(c=c, rows=rows, d=d, q=q, v=v, k=k, g=g, tri=tri, keep=keep, rev=rev))
        for ch in chains:
            ch["b"] = _chunk_cumsum(ch["tri"], ch["g"])
        for ch in chains:
            ch["a"] = scores(ch["q"], ch["k"], ch["b"], ch["rev"])
        for ch in chains:
            d, rows, b = ch["d"], ch["rows"], ch["b"]
            a = jnp.where(ch["keep"], ch["a"], 0.0).astype(BF16)
            o_scr[d, rows, :] = _dot(a, ch["v"])
            qd_scr[d, rows, :] = (ch["q"] * jnp.exp(b)).astype(BF16)
            edge = b[0:1, :] if ch["rev"] else b[CHUNK - 1:CHUNK, :]
            ut_scr[d, ch["c"]] = _dot_tn(ch["v"], (ch["k"] * jnp.exp(edge - b)).astype(BF16))
            dec_scr[d, ch["c"]] = jnp.exp(edge)
        return carry

    @pl.when(safe)
    def _():
        lax.fori_loop(0, n_chunks // per_it, functools.partial(local, scores=_intra_scores), 0)

    @pl.when(jnp.logical_not(safe))
    def _():
        lax.fori_loop(0, n_chunks // per_it, functools.partial(local, scores=_intra_scores_exact), 0)

    def scan(j, carry):
        new = []
        for d, st in enumerate(carry):
            c = j if d == 0 else n_chunks - 1 - j
            upd = ut_scr[d, c]
            ut_scr[d, c] = st
            new.append(st * dec_scr[d, c] + upd)
        return tuple(new)

    st_f, st_b = lax.fori_loop(0, n_chunks, scan, (st_f0, st_b0), unroll=2)

    per_it3 = 4

    def inter(it, carry):
        prods = []
        for u in range(per_it3):
            c = it * per_it3 + u
            rows = pl.ds(pl.multiple_of(c * CHUNK, CHUNK), CHUNK)
            for d in range(2):
                prods.append((d, rows, _dot_nt(qd_scr[d, rows, :], ut_scr[d, c].astype(BF16))))
        for d, rows, p in prods:
            o_scr[d, rows, :] += p
        return carry

    lax.fori_loop(0, n_chunks // per_it3, inter, 0)
    if emit_state:
        so_ref[0] = st_f.T
        so_ref[1] = st_b.T

    nrm = nrm_ref[...]
    piece = 256 if seq_len % 256 == 0 else CHUNK

    def fin(p, carry):
        r0 = pl.multiple_of(p * piece, piece)
        o = o_scr[0, pl.ds(r0, piece), :] + o_scr[1, pl.ds(r0, piece), :]
        o = o * lax.rsqrt(jnp.mean(o * o, axis=-1, keepdims=True) + EPS) * nrm
        gate = g_ref[pl.ds(r0, piece), :].astype(F32)
        o_ref[pl.ds(r0, piece), :] = (o * _silu(gate)).astype(BF16)
        return carry

    lax.fori_loop(0, seq_len // piece, fin, 0)


def _hgrn(proj, lbc, nrm, seq_len, n_seq, row_blk0, state_in=None, layer=0, emit_state=False):
    kern = functools.partial(_hgrn_kernel, seq_len=seq_len, has_init=state_in is not None, emit_state=emit_state)

    def col(sec):
        return pl.BlockSpec((None, seq_len, HEAD_DIM), lambda b, h: (sec * H_R + h, row_blk0 + b, 0))

    in_specs = [col(0), col(1), col(2), col(3), col(4),
                pl.BlockSpec((8, HEAD_DIM), lambda b, h: (0, h)),
                pl.BlockSpec((1, HEAD_DIM), lambda b, h: (0, 0))]
    args = [proj, proj, proj, proj, proj, lbc, nrm]
    if state_in is not None:
        in_specs.append(pl.BlockSpec((None, None, 2, None, HEAD_DIM, HEAD_DIM),
                                     lambda b, h: (b, layer, 0, h, 0, 0)))
        args.append(state_in)
    out_specs = [pl.BlockSpec((seq_len, HEAD_DIM), lambda b, h: (b, h))]
    out_shape = [jax.ShapeDtypeStruct((n_seq * seq_len, W_R), BF16)]
    if emit_state:
        out_specs.append(pl.BlockSpec((None, 2, None, HEAD_DIM, HEAD_DIM), lambda b, h: (b, 0, h, 0, 0)))
        out_shape.append(jax.ShapeDtypeStruct((n_seq, 2, H_R, HEAD_DIM, HEAD_DIM), F32))
    res = pl.pallas_call(
        kern,
        grid=(n_seq, H_R),
        in_specs=in_specs,
        out_specs=out_specs,
        out_shape=out_shape,
        scratch_shapes=[pltpu.VMEM((2, seq_len, HEAD_DIM), F32),
                        pltpu.VMEM((2, seq_len, HEAD_DIM), BF16),
                        pltpu.VMEM((2, seq_len // CHUNK, HEAD_DIM, HEAD_DIM), F32),
                        pltpu.VMEM((2, seq_len // CHUNK, 1, HEAD_DIM), F32)],
        compiler_params=_cparams(("arbitrary", "arbitrary")),
        name="hgrn_state" if emit_state else "hgrn",
    )(*args)
    return res


def _ctx_attn_kernel(q_ref, k_ref, v_ref, o_ref, ko_ref, vo_ref):
    heads = range(H_A)
    scores = [_dot_nt(q_ref[h], k_ref[h]) * (HEAD_DIM ** -0.5) for h in heads]
    probs, dens = [], []
    for s in scores:
        p = jnp.exp(s - jnp.max(s, axis=-1, keepdims=True))
        dens.append(jnp.sum(p, axis=-1, keepdims=True))
        probs.append(p.astype(BF16))
    for h in heads:
        o = _dot(probs[h], v_ref[h]) / dens[h]
        o_ref[:, h * HEAD_DIM:(h + 1) * HEAD_DIM] = o.astype(BF16)
        ko_ref[h] = k_ref[h].astype(F32)
        vo_ref[h] = v_ref[h].astype(F32)


def _ctx_attention(proj, seq_len, n_seq):
    def col(sec):
        return pl.BlockSpec((H_A, seq_len, HEAD_DIM), lambda b: ((5 * H_R + sec * H_A) // H_A, b, 0))

    cache_spec = pl.BlockSpec((None, H_A, seq_len, HEAD_DIM), lambda b: (b, 0, 0, 0))
    cache_shape = jax.ShapeDtypeStruct((n_seq, H_A, seq_len, HEAD_DIM), F32)
    return pl.pallas_call(
        _ctx_attn_kernel,
        grid=(n_seq,),
        in_specs=[col(0), col(1), col(2)],
        out_specs=[pl.BlockSpec((seq_len, W_A), lambda b: (b, 0)), cache_spec, cache_spec],
        out_shape=[jax.ShapeDtypeStruct((n_seq * seq_len, W_A), BF16), cache_shape, cache_shape],
        compiler_params=_cparams(("arbitrary",)),
        name="ctx_attention",
    )(proj, proj, proj)


def _rope(x, cos, sin_lo, sin_hi):
    return (x * cos + pltpu.roll(x, HEAD_DIM - HEAD_DIM // 4, axis=1) * sin_lo
            + pltpu.roll(x, HEAD_DIM // 4, axis=1) * sin_hi)


def _nbr_attn_kernel(q_ref, k_ref, v_ref, kc_ref, vc_ref, bias_ref, cos_ref, slo_ref, shi_ref, o_ref, kr_scr,
                     *, n_rows):
    rows_pc = 256

    def rope_k(p, carry):
        r0 = pl.multiple_of(p * rows_pc, rows_pc)
        sl = pl.ds(r0, rows_pc)
        kr_scr[sl, :] = _rope(k_ref[sl, :].astype(F32), cos_ref[sl, :], slo_ref[sl, :], shi_ref[sl, :]).astype(BF16)
        return carry

    lax.fori_loop(0, (n_rows * GRID_W) // rows_pc, rope_k, 0)
    kc = kc_ref[...].astype(BF16)
    vc = vc_ref[...].astype(BF16)
    scale = HEAD_DIM ** -0.5
    win = WIN_R * GRID_W

    per_it = 8 if n_rows % 8 == 0 else 1

    def rows_step(it, carry):
        items = []
        for u in range(per_it):
            r = it * per_it + u
            rs = jnp.clip(r - WIN_R // 2, 0, n_rows - WIN_R)
            qs = pl.ds(pl.multiple_of(r * GRID_W, GRID_W), GRID_W)
            ks = pl.ds(pl.multiple_of(rs * GRID_W, GRID_W), win)
            q = _rope(q_ref[qs, :].astype(F32), cos_ref[qs, :], slo_ref[qs, :], shi_ref[qs, :])
            items.append(dict(qs=qs, ks=ks, dl=rs - r + WIN_R - 1, q=(q * scale).astype(BF16)))
        for x in items:
            x["s_loc"] = _dot_nt(x["q"], kr_scr[x["ks"], :]) + bias_ref[x["dl"]]
            x["s_ctx"] = _dot_nt(x["q"], kc)
        for x in items:
            m = jnp.maximum(jnp.max(x["s_loc"], axis=-1, keepdims=True), jnp.max(x["s_ctx"], axis=-1, keepdims=True))
            p_loc = jnp.exp(x["s_loc"] - m)
            p_ctx = jnp.exp(x["s_ctx"] - m)
            x["den"] = jnp.sum(p_loc, axis=-1, keepdims=True) + jnp.sum(p_ctx, axis=-1, keepdims=True)
            x["p_loc"] = p_loc.astype(BF16)
            x["p_ctx"] = p_ctx.astype(BF16)
        for x in items:
            o = _dot(x["p_loc"], v_ref[x["ks"], :]) + _dot(x["p_ctx"], vc)
            o_ref[x["qs"], :] = (o / x["den"]).astype(BF16)
        return carry

    lax.fori_loop(0, n_rows // per_it, rows_step, 0)


def _nbr_attention(proj, cache_k, cache_v, bias_tab, rope_tabs, layer, seq_len, n_seq, row_blk0):
    n_rows = seq_len // GRID_W
    past = cache_k.shape[3]
    kern = functools.partial(_nbr_attn_kernel, n_rows=n_rows)

    def col(sec):
        return pl.BlockSpec((None, seq_len, HEAD_DIM), lambda b, h: (5 * H_R + sec * H_A + h, row_blk0 + b, 0))

    cache_spec = pl.BlockSpec((None, None, None, past, HEAD_DIM), lambda b, h: (b, layer, h, 0, 0))
    tab_spec = pl.BlockSpec((seq_len, HEAD_DIM), lambda b, h: (0, 0))
    return pl.pallas_call(
        kern,
        grid=(n_seq, H_A),
        in_specs=[col(0), col(1), col(2), cache_spec, cache_spec,
                  pl.BlockSpec((None, WIN_R, GRID_W, WIN_R * GRID_W), lambda b, h: (h, 0, 0, 0)),
                  tab_spec, tab_spec, tab_spec],
        out_specs=pl.BlockSpec((seq_len, HEAD_DIM), lambda b, h: (b, h)),
        out_shape=jax.ShapeDtypeStruct((n_seq * seq_len, W_A), BF16),
        scratch_shapes=[pltpu.VMEM((seq_len, HEAD_DIM), BF16)],
        compiler_params=_cparams(("arbitrary", "arbitrary")),
        name="nbr_attention",
    )(proj, proj, proj, cache_k, cache_v, bias_tab, *rope_tabs)


def _first_argmax_rows(vals, ids, sentinel):
    m = vals[0]
    for v in vals[1:]:
        m = jnp.maximum(m, v)
    m = jnp.max(m, axis=0, keepdims=True)
    best = None
    for v, i in zip(vals, ids):
        cand = jnp.min(jnp.where(v == m, i, sentinel), axis=0, keepdims=True)
        best = cand if best is None else jnp.minimum(best, cand)
    return m, best


def _route(s, sb):
    n = s.shape[1]
    iota8 = lax.broadcasted_iota(jnp.int32, (GROUP_SIZE, n), 0)
    neg_inf = -jnp.inf
    groups = [sb[g * GROUP_SIZE:(g + 1) * GROUP_SIZE, :] for g in range(N_GROUPS)]
    gscore = jnp.zeros((N_GROUPS, n), F32)
    for g, xg in enumerate(groups):
        m1, i1 = _first_argmax_rows([xg], [iota8], GROUP_SIZE)
        m2 = jnp.max(jnp.where(iota8 == i1, neg_inf, xg), axis=0, keepdims=True)
        gscore = jnp.where(iota8 == g, m1 + m2, gscore)
    gsel = jnp.zeros((N_GROUPS, n), F32)
    cur = gscore
    for _ in range(TOPK_GROUPS):
        _, i = _first_argmax_rows([cur], [iota8], N_GROUPS)
        hit = iota8 == i
        gsel = jnp.where(hit, 1.0, gsel)
        cur = jnp.where(hit, neg_inf, cur)
    cur = [jnp.where(gsel[g:g + 1, :] > 0.5, xg, neg_inf) for g, xg in enumerate(groups)]
    ids = [iota8 + g * GROUP_SIZE for g in range(N_GROUPS)]
    sel = [jnp.zeros((GROUP_SIZE, n), jnp.bool_) for _ in range(N_GROUPS)]
    for _ in range(TOP_K):
        _, i = _first_argmax_rows(cur, ids, N_EXPERTS)
        hits = [idg == i for idg in ids]
        sel = [jnp.logical_or(a, h) for a, h in zip(sel, hits)]
        cur = [jnp.where(h, neg_inf, c) for c, h in zip(cur, hits)]
    return sel


def _outproj_kernel(orp_ref, ors_ref, oap_ref, oas_ref, x_ref, mod_ref, w_ref, lng_ref, lnb_ref, wr_ref, rb_ref,
                    x1_ref, h2_ref, wt_ref, sel_ref, nb_ref, *, alpha, n_prompt_tiles):
    from_prompt = pl.program_id(0) < n_prompt_tiles
    n = MOE_TILE
    tiles = [slice(i * n, (i + 1) * n) for i in range(x_ref.shape[0] // n)]
    mixes = []
    for sl in tiles:
        o_r = jnp.where(from_prompt, orp_ref[sl, :], ors_ref[sl, :])
        o_a = jnp.where(from_prompt, oap_ref[sl, :], oas_ref[sl, :])
        mixes.append(_dot(o_r, w_ref[0:W_R, :]) + _dot(o_a, w_ref[W_R:W_R + W_A, :]))
    g1 = mod_ref[:, 2 * D_MODEL:3 * D_MODEL]
    sh2 = mod_ref[:, 3 * D_MODEL:4 * D_MODEL]
    sc2 = mod_ref[:, 4 * D_MODEL:5 * D_MODEL]
    scores = []
    for sl, mix in zip(tiles, mixes):
        x1 = _layer_norm(alpha * x_ref[sl, :] + g1 * mix) * lng_ref[0:1, :] + lnb_ref[0:1, :]
        x1_ref[sl, :] = x1
        h2 = (_layer_norm(x1) * (1.0 + sc2) + sh2).astype(BF16)
        h2_ref[sl, :] = h2
        scores.append(jax.nn.sigmoid(_dot_nt(wr_ref[...], h2)))
    for i, (sl, s) in enumerate(zip(tiles, scores)):
        sel = _route(s, s + rb_ref[...])
        wsel = [jnp.where(sel[g], s[g * GROUP_SIZE:(g + 1) * GROUP_SIZE, :], 0.0) for g in range(N_GROUPS)]
        tot = wsel[0]
        for w in wsel[1:]:
            tot = tot + w
        tot = jnp.sum(tot, axis=0, keepdims=True)
        for g in range(N_GROUPS):
            rows = slice(g * GROUP_SIZE, (g + 1) * GROUP_SIZE)
            wt_ref[rows, sl] = wsel[g] / tot * ROUTED_SCALE
            sel_ref[rows, sl] = jnp.where(sel[g], 1.0, 0.0)
        cnt = _dot_nt(jnp.ones((8, n), BF16), sel_ref[:, sl].astype(BF16))
        nb_ref[i] = jnp.floor((cnt + (ROW_BLK - 1)) * (1.0 / ROW_BLK))


def _outproj(or_p, or_s, oa_p, oa_s, x, mod_l, w_out_l, ln_g_l, ln_b_l, wr_t, rbias, alpha, cond_of_tile):
    t = x.shape[0]
    per_step = 2
    tm = per_step * MOE_TILE
    nt = t // tm
    n_p = or_p.shape[0] // tm
    nc = mod_l.shape[0]
    kern = functools.partial(_outproj_kernel, alpha=alpha, n_prompt_tiles=n_p)
    row = lambda i: (i, 0)
    const = lambda i: (0, 0)
    prompt_row = lambda i: (jnp.minimum(i, n_p - 1), 0)
    sample_row = lambda i: (jnp.maximum(i - n_p, 0), 0)
    return pl.pallas_call(
        kern,
        grid=(nt,),
        in_specs=[
            pl.BlockSpec((tm, W_R), prompt_row),
            pl.BlockSpec((tm, W_R), sample_row),
            pl.BlockSpec((tm, W_A), prompt_row),
            pl.BlockSpec((tm, W_A), sample_row),
            pl.BlockSpec((tm, D_MODEL), row),
            pl.BlockSpec((None, 1, 6 * D_MODEL), lambda i: (cond_of_tile(i), 0, 0)),
            pl.BlockSpec((W_R + W_A, D_MODEL), const),
            pl.BlockSpec((2, D_MODEL), const),
            pl.BlockSpec((2, D_MODEL), const),
            pl.BlockSpec((N_EXPERTS, D_MODEL), const),
            pl.BlockSpec((N_EXPERTS, 1), const),
        ],
        out_specs=[
            pl.BlockSpec((tm, D_MODEL), row),
            pl.BlockSpec((tm, D_MODEL), row),
            pl.BlockSpec((N_EXPERTS, tm), lambda i: (0, i)),
            pl.BlockSpec((N_EXPERTS, tm), lambda i: (0, i)),
            pl.BlockSpec((per_step, 8, N_EXPERTS), lambda i: (i, 0, 0)),
        ],
        out_shape=[
            jax.ShapeDtypeStruct((t, D_MODEL), F32),
            jax.ShapeDtypeStruct((t, D_MODEL), BF16),
            jax.ShapeDtypeStruct((N_EXPERTS, t), F32),
            jax.ShapeDtypeStruct((N_EXPERTS, t), F32),
            jax.ShapeDtypeStruct((nt * per_step, 8, N_EXPERTS), F32),
        ],
        compiler_params=_cparams(("arbitrary",)),
        name="outproj_router",
    )(or_p, or_s, oa_p, oa_s, x, mod_l.reshape(nc, 1, 6 * D_MODEL), w_out_l, ln_g_l, ln_b_l, wr_t, rbias)


def _sorted_layout(sel):
    n = sel.shape[1]
    selb = sel.astype(BF16)
    ti = lax.broadcasted_iota(jnp.int32, (n, n), 0)
    tj = lax.broadcasted_iota(jnp.int32, (n, n), 1)
    rank = _dot(selb, jnp.where(ti < tj, 1.0, 0.0).astype(BF16))
    cnt = _dot(selb, jnp.ones((n, n), BF16))
    nb = jnp.floor((cnt + (ROW_BLK - 1)) * (1.0 / ROW_BLK))
    ei = lax.broadcasted_iota(jnp.int32, (N_EXPERTS, N_EXPERTS), 0)
    ej = lax.broadcasted_iota(jnp.int32, (N_EXPERTS, N_EXPERTS), 1)
    boff = _dot(jnp.where(ej < ei, 1.0, 0.0).astype(BF16), nb.astype(BF16))
    eye = lax.broadcasted_iota(jnp.int32, (N_EXPERTS, n), 0) == lax.broadcasted_iota(jnp.int32, (N_EXPERTS, n), 1)
    ones8 = jnp.ones((8, N_EXPERTS), BF16)
    boff_row = _dot(ones8, jnp.where(eye, boff, 0.0).astype(BF16))[0:1, 0:N_EXPERTS]
    nb_row = _dot(ones8, jnp.where(eye, nb, 0.0).astype(BF16))[0:1, 0:N_EXPERTS]
    rankp = jnp.where(sel > 0.5, rank + 1.0, 0.0)
    return rankp, boff, boff_row, nb_row


def _row_expert_onehot(row0, n_rows, boff_row, nb_row):
    blk = (lax.broadcasted_iota(jnp.int32, (n_rows, N_EXPERTS), 0) + row0) // ROW_BLK
    blk = blk.astype(F32)
    return jnp.where(jnp.logical_and(blk >= boff_row, blk < boff_row + nb_row), 1.0, 0.0).astype(BF16)


def _sort_onehots(chunks, keys, boff_row, nb_row):
    n = keys.shape[1] // 2
    onehots = [_row_expert_onehot(c * DISP_ROWS, DISP_ROWS, boff_row, nb_row) for c in chunks]
    looked = [_dot(e, keys) for e in onehots]
    out = []
    for c, qk in zip(chunks, looked):
        row = lax.broadcasted_iota(jnp.int32, (DISP_ROWS, n), 0) + (c * DISP_ROWS + 1)
        out.append(jnp.where(qk[:, 0:n] == row.astype(F32) - ROW_BLK * qk[:, n:2 * n], 1.0, 0.0).astype(BF16))
    return out


def _dispatch_kernel(used_ref, h2_ref, sel_ref, wt_ref, xs_ref, xa_scr):
    s = pl.program_id(0)
    n = h2_ref.shape[0]
    rankp, boff, boff_row, nb_row = _sorted_layout(sel_ref[...])
    keys = jnp.concatenate([rankp, boff], axis=1).astype(BF16)
    ti = lax.broadcasted_iota(jnp.int32, (n, n), 0)
    tj = lax.broadcasted_iota(jnp.int32, (n, n), 1)
    eye = jnp.where(ti == tj, 1.0, 0.0).astype(BF16)
    w1, w2, w3 = _split3(wt_ref[...])
    xa_scr[:, 0:D_MODEL] = h2_ref[...]
    xa_scr[:, D_MODEL:D_MODEL + 128] = _dot_nt(eye, jnp.concatenate([w1, w2], axis=0)).astype(BF16)
    xa_scr[:, D_MODEL + 128:XS_COLS] = _dot_nt(eye, jnp.concatenate([w3, jnp.zeros_like(w3)], axis=0)).astype(BF16)

    def emit(chunks):
        for c, g in zip(chunks, _sort_onehots(chunks, keys, boff_row, nb_row)):
            xs_ref[c * DISP_ROWS:(c + 1) * DISP_ROWS, :] = _dot(g, xa_scr[...]).astype(BF16)

    always = (MOE_TILE * TOP_K) // DISP_ROWS
    emit(tuple(range(always)))
    for c in range(always, TILE_ROWS // DISP_ROWS):
        @pl.when(c * DISP_ROWS < used_ref[s])
        def _():
            emit((c,))

        @pl.when(c * DISP_ROWS >= used_ref[s])
        def _():
            xs_ref[c * DISP_ROWS:(c + 1) * DISP_ROWS, :] = jnp.zeros((DISP_ROWS, XS_COLS), BF16)


def _dispatch(used_rows, h2, sel_t, w_t):
    t = h2.shape[0]
    nt = t // MOE_TILE
    return pl.pallas_call(
        _dispatch_kernel,
        grid_spec=pltpu.PrefetchScalarGridSpec(
            num_scalar_prefetch=1,
            grid=(nt,),
            in_specs=[
                pl.BlockSpec((MOE_TILE, D_MODEL), lambda s, u: (s, 0)),
                pl.BlockSpec((N_EXPERTS, MOE_TILE), lambda s, u: (0, s)),
                pl.BlockSpec((N_EXPERTS, MOE_TILE), lambda s, u: (0, s)),
            ],
            out_specs=pl.BlockSpec((TILE_ROWS, XS_COLS), lambda s, u: (s, 0)),
            scratch_shapes=[pltpu.VMEM((MOE_TILE, XS_COLS), BF16)],
        ),
        out_shape=jax.ShapeDtypeStruct((nt * TILE_ROWS, XS_COLS), BF16),
        compiler_params=_cparams(("arbitrary",)),
        name="moe_dispatch",
    )(used_rows, h2, sel_t, w_t)


def _expert_kernel(bstart_ref, bsrc_ref, xs_hbm, w1_ref, w2_ref, ys_hbm, w1_scr, w2_scr, x_buf, y_buf, sem_in, sem_out):
    e = pl.program_id(0)
    b0 = bstart_ref[e]
    n_blk = bstart_ref[e + 1] - b0
    n_groups = (n_blk + EXP_GROUP - 1) // EXP_GROUP

    def in_copy(j, slot, k):
        src = pl.multiple_of(bsrc_ref[j] * ROW_BLK, ROW_BLK)
        row = k * ROW_BLK if isinstance(k, int) else pl.multiple_of(k * ROW_BLK, ROW_BLK)
        return pltpu.make_async_copy(xs_hbm.at[pl.ds(src, ROW_BLK), :],
                                     x_buf.at[slot, pl.ds(row, ROW_BLK), :], sem_in.at[slot])

    def out_copy(j, slot, k):
        dst = pl.multiple_of(bsrc_ref[j] * ROW_BLK, ROW_BLK)
        row = k * ROW_BLK if isinstance(k, int) else pl.multiple_of(k * ROW_BLK, ROW_BLK)
        return pltpu.make_async_copy(y_buf.at[slot, pl.ds(row, ROW_BLK), :],
                                     ys_hbm.at[pl.ds(dst, ROW_BLK), pl.ds(0, D_MODEL)], sem_out.at[slot])

    def for_blocks(gi, fn):
        k0 = gi * EXP_GROUP
        cnt = jnp.minimum(n_blk - k0, EXP_GROUP)

        @pl.when(cnt == EXP_GROUP)
        def _():
            for k in range(EXP_GROUP):
                fn(b0 + k0 + k, k)

        @pl.when(cnt < EXP_GROUP)
        def _():
            def body(k, carry):
                fn(b0 + k0 + k, k)
                return carry

            lax.fori_loop(0, cnt, body, 0)

    @pl.when(e == 0)
    def _():
        x_buf[...] = jnp.zeros(x_buf.shape, BF16)

    @pl.when(n_groups > 0)
    def _():
        for_blocks(0, lambda j, k: in_copy(j, 0, k).start())

    w1_scr[...] = w1_ref[...].astype(BF16)
    w2_scr[...] = w2_ref[...].astype(BF16)

    def group(gi, carry):
        slot = gi % 2

        @pl.when(gi + 1 < n_groups)
        def _():
            for_blocks(gi + 1, lambda j, k: in_copy(j, 1 - slot, k).start())

        for_blocks(gi, lambda j, k: in_copy(j, slot, k).wait())

        @pl.when(gi >= 2)
        def _():
            for_blocks(gi - 2, lambda j, k: out_copy(j, slot, k).wait())

        def ffn(rows):
            x = x_buf[slot, 0:rows, 0:D_MODEL]
            lane = lax.broadcasted_iota(jnp.int32, (rows, W_LANES), 1)
            mine = (lane & (N_EXPERTS - 1)) == e
            wrow = jnp.sum(jnp.where(mine, x_buf[slot, 0:rows, D_MODEL:XS_COLS].astype(F32), 0.0),
                           axis=1, keepdims=True)
            h = _dot(x, w1_scr[...])
            act = (_silu(h[:, 0:D_EXPERT]) * h[:, D_EXPERT:2 * D_EXPERT]).astype(BF16)
            y_buf[slot, 0:rows, :] = (_dot(act, w2_scr[...]) * wrow).astype(BF16)

        full_rows = EXP_GROUP * ROW_BLK
        short = n_blk - gi * EXP_GROUP <= EXP_GROUP // 2

        @pl.when(jnp.logical_not(short))
        def _():
            ffn(full_rows)

        @pl.when(short)
        def _():
            ffn(full_rows // 2)

        for_blocks(gi, lambda j, k: out_copy(j, slot, k).start())
        return carry

    lax.fori_loop(0, n_groups, group, 0)

    for back in (2, 1):
        @pl.when(n_groups >= back)
        def _():
            g_last = n_groups - back
            for_blocks(g_last, lambda j, k: out_copy(j, g_last % 2, k).wait())


def _experts(bstart, bsrc, xs, w_e_in, w_e_out, layer):
    rows = EXP_GROUP * ROW_BLK
    return pl.pallas_call(
        _expert_kernel,
        grid_spec=pltpu.PrefetchScalarGridSpec(
            num_scalar_prefetch=2,
            grid=(N_EXPERTS,),
            in_specs=[
                pl.BlockSpec(memory_space=pl.ANY),
                pl.BlockSpec((None, None, D_MODEL, 2 * D_EXPERT), lambda e, bs, br: (layer, e, 0, 0)),
                pl.BlockSpec((None, None, D_EXPERT, D_MODEL), lambda e, bs, br: (layer, e, 0, 0)),
            ],
            out_specs=pl.BlockSpec(memory_space=pl.ANY),
            scratch_shapes=[
                pltpu.VMEM((D_MODEL, 2 * D_EXPERT), BF16),
                pltpu.VMEM((D_EXPERT, D_MODEL), BF16),
                pltpu.VMEM((2, rows, XS_COLS), BF16),
                pltpu.VMEM((2, rows, D_MODEL), BF16),
                pltpu.SemaphoreType.DMA((2,)),
                pltpu.SemaphoreType.DMA((2,)),
            ],
        ),
        out_shape=jax.ShapeDtypeStruct(xs.shape, xs.dtype),
        input_output_aliases={2: 0},
        compiler_params=_cparams(("arbitrary",)),
        name="moe_experts",
    )(bstart, bsrc, xs, w_e_in, w_e_out)


def _combine_kernel(used_ref, ys_ref, sel_ref, h2_ref, x1_ref, mod_ref, wsi_ref, wso_ref, lng_ref, lnb_ref,
                    o_ref, acc_scr, keyt_scr, row_scr, *, alpha):
    s = pl.program_id(0)
    half = pl.program_id(1)
    n = h2_ref.shape[0]
    per_step = TILE_ROWS // DISP_ROWS // COMB_SPLIT
    always = (MOE_TILE * TOP_K) // DISP_ROWS

    @pl.when(half == 0)
    def _():
        rankp, _, boff_row, nb_row = _sorted_layout(sel_ref[...])
        ti = lax.broadcasted_iota(jnp.int32, (n, n), 0)
        tj = lax.broadcasted_iota(jnp.int32, (n, n), 1)
        eye = jnp.where(ti == tj, 1.0, 0.0).astype(BF16)
        keyt_scr[0:n, :] = _dot_nt(eye, rankp.astype(BF16)).astype(BF16)
        keyt_scr[n:2 * n, :] = jnp.broadcast_to(boff_row, (n, N_EXPERTS)).astype(BF16)
        row_scr[0:1, :] = boff_row
        row_scr[1:2, :] = nb_row
        h = _dot(h2_ref[...], wsi_ref[...])
        act = (_silu(h[:, 0:D_SHARED]) * h[:, D_SHARED:2 * D_SHARED]).astype(BF16)
        acc_scr[...] = _dot(act, wso_ref[...])

    def unsort(chunks):
        onehots = [_row_expert_onehot(g * DISP_ROWS, DISP_ROWS, row_scr[0:1, :], row_scr[1:2, :]) for g in chunks]
        looked = [_dot_nt(keyt_scr[...], e) for e in onehots]
        total = None
        for g, qk in zip(chunks, looked):
            row = lax.broadcasted_iota(jnp.int32, (n, DISP_ROWS), 1) + (g * DISP_ROWS + 1)
            c = jnp.where(qk[0:n] == row.astype(F32) - ROW_BLK * qk[n:2 * n], 1.0, 0.0).astype(BF16)
            lo = (g % per_step) * DISP_ROWS
            part = _dot(c, ys_ref[lo:lo + DISP_ROWS, :])
            total = part if total is None else total + part
        acc_scr[...] += total

    for step in range(COMB_SPLIT):
        @pl.when(half == step)
        def _():
            mine = range(step * per_step, (step + 1) * per_step)
            sure = tuple(g for g in mine if g < always)
            if sure:
                unsort(sure)
            for g in mine:
                if g >= always:
                    @pl.when(g * DISP_ROWS < used_ref[s])
                    def _():
                        unsort((g,))

    @pl.when(half == COMB_SPLIT - 1)
    def _():
        g2 = mod_ref[:, 5 * D_MODEL:6 * D_MODEL]
        y = _layer_norm(alpha * x1_ref[...] + g2 * acc_scr[...])
        o_ref[...] = y * lng_ref[1:2, :] + lnb_ref[1:2, :]


def _combine(used_rows, ys, sel_t, h2, x1, mod_l, w_sh_in_l, w_sh_out_l, ln_g_l, ln_b_l, alpha, cond_of_tile):
    t = h2.shape[0]
    nt = t // MOE_TILE
    nrc = COMB_SPLIT
    comb_rows = TILE_ROWS // COMB_SPLIT
    nc = mod_l.shape[0]
    kern = functools.partial(_combine_kernel, alpha=alpha)
    row = lambda s, r, u: (s, 0)
    const = lambda s, r, u: (0, 0)
    return pl.pallas_call(
        kern,
        grid_spec=pltpu.PrefetchScalarGridSpec(
            num_scalar_prefetch=1,
            grid=(nt, nrc),
            in_specs=[
                pl.BlockSpec((comb_rows, D_MODEL), lambda s, r, u: (s * nrc + r, 0)),
                pl.BlockSpec((N_EXPERTS, MOE_TILE), lambda s, r, u: (0, s)),
                pl.BlockSpec((MOE_TILE, D_MODEL), row),
                pl.BlockSpec((MOE_TILE, D_MODEL), row),
                pl.BlockSpec((None, 1, 6 * D_MODEL), lambda s, r, u: (cond_of_tile(s), 0, 0)),
                pl.BlockSpec((D_MODEL, 2 * D_SHARED), const),
                pl.BlockSpec((D_SHARED, D_MODEL), const),
                pl.BlockSpec((2, D_MODEL), const),
                pl.BlockSpec((2, D_MODEL), const),
            ],
            out_specs=pl.BlockSpec((MOE_TILE, D_MODEL), row),
            scratch_shapes=[
                pltpu.VMEM((MOE_TILE, D_MODEL), F32),
                pltpu.VMEM((2 * MOE_TILE, N_EXPERTS), BF16),
                pltpu.VMEM((8, N_EXPERTS), F32),
            ],
        ),
        out_shape=jax.ShapeDtypeStruct((t, D_MODEL), F32),
        compiler_params=_cparams(("arbitrary", "arbitrary")),
        name="moe_combine",
    )(used_rows, ys, sel_t, h2, x1, mod_l.reshape(nc, 1, 6 * D_MODEL), w_sh_in_l, w_sh_out_l, ln_g_l, ln_b_l)


def _block_lists(nb):
    nt = nb.shape[0]
    cum = jnp.cumsum(nb, axis=1)
    boff = cum - nb
    used_blk = cum[:, -1]
    per_e = jnp.sum(nb, axis=0)
    bstart = jnp.concatenate([jnp.zeros((1,), jnp.int32), jnp.cumsum(per_e).astype(jnp.int32)])
    pref = jnp.cumsum(nb, axis=0) - nb
    lb = jnp.arange(TILE_BLKS, dtype=jnp.int32)
    owner = (lb[None, :, None] >= boff[:, None, :]) & (lb[None, :, None] < cum[:, None, :])
    base = bstart[None, :-1] + pref - boff
    pos = jnp.sum(jnp.where(owner, base[:, None, :], 0), axis=-1) + lb[None, :]
    total = nt * TILE_BLKS
    pos = jnp.where(jnp.any(owner, axis=-1), pos, total)
    src = (jnp.arange(nt, dtype=jnp.int32)[:, None] * TILE_BLKS + lb[None, :]).reshape(-1)
    bsrc = jnp.zeros((total,), jnp.int32).at[pos.reshape(-1)].set(src, mode="drop")
    return (used_blk * ROW_BLK).astype(jnp.int32), bstart, bsrc


def _lower_bound_consts(lb_raw):
    p = jax.nn.softmax(lb_raw.astype(F32), axis=0)
    cs = jnp.cumsum(p, axis=0)
    lb = jnp.clip(jnp.concatenate([jnp.zeros_like(cs[:1]), cs[:-1]], axis=0), 0.0, LB_MAX)
    rows = jnp.stack([jnp.log(lb[:, 0]), jnp.log1p(-lb[:, 0]), 1.0 - lb[:, 0],
                      jnp.log(lb[:, 1]), jnp.log1p(-lb[:, 1]), 1.0 - lb[:, 1]], axis=1)
    return jnp.concatenate([rows, jnp.zeros((rows.shape[0], 2, W_R), F32)], axis=1)


def _rope_tables(seq_len):
    t = jnp.arange(seq_len)
    half = HEAD_DIM // 2
    nf = half // 2
    inv = ROPE_THETA ** (-jnp.arange(nf, dtype=F32) / nf)

    def tabs(pos):
        ang = pos[:, None].astype(F32) * inv[None, :]
        cos = jnp.concatenate([jnp.cos(ang), jnp.cos(ang)], -1)
        sin = jnp.sin(ang)
        zero = jnp.zeros_like(sin)
        return cos, jnp.concatenate([-sin, zero], -1), jnp.concatenate([zero, sin], -1)

    row = tabs(t // GRID_W)
    colt = tabs(t % GRID_W)
    return tuple(jnp.concatenate([a, b], -1) for a, b in zip(row, colt))


def _bias_tables(rpb):
    cq = jnp.arange(GRID_W)
    cs = jnp.clip(cq - WIN_C // 2, 0, GRID_W - WIN_C)
    valid = (cq[None, :] >= cs[:, None]) & (cq[None, :] < cs[:, None] + WIN_C)
    coff = jnp.clip(cq[None, :] - cq[:, None], -(WIN_C - 1), WIN_C - 1) + WIN_C - 1
    roff = jnp.arange(WIN_R)[:, None] + jnp.arange(WIN_R)[None, :]
    pick_c = (coff[None] == jnp.arange(2 * WIN_C - 1)[:, None, None]).astype(F32)
    pick_r = (roff[None] == jnp.arange(2 * WIN_R - 1)[:, None, None]).astype(F32)
    hi = lax.Precision.HIGHEST
    bias_c = jnp.einsum('dhrc,cqw->dhrqw', rpb.astype(F32), pick_c, precision=hi)
    tab = jnp.einsum('rlk,dhrqw->dhlqkw', pick_r, bias_c, precision=hi)
    tab = jnp.where(valid[None, None, None, :, None, :], tab, NEG_BIG)
    return tab.reshape(tab.shape[0], H_A, WIN_R, GRID_W, WIN_R * GRID_W)


def kernel(x_prompt, x_sample, cache_k, cache_v, state_hgrn, c, c_ctx, w_ada, b_ada, w_in, w_out, lb_raw, hgrn_norm,
           rpb, ln_g, ln_b, w_router, router_bias, w_e_in, w_e_out, w_sh_in, w_sh_out):
    n_p, seq, d = x_prompt.shape
    n_s, dseq, _ = x_sample.shape
    depth = w_in.shape[0]
    t_p, t_s = n_p * seq, n_s * dseq
    t = t_p + t_s
    alpha = (2 * depth) ** 0.25
    assert d == D_MODEL and seq % MOE_TILE == 0 and dseq % (2 * MOE_TILE) == 0 and t_p % dseq == 0
    assert dseq % GRID_W == 0 and dseq // GRID_W >= WIN_R

    tm_in = next(m for m in (1024, 512, 256) if t_p % m == 0 and dseq % m == 0)

    def cond_of_tile(tile_rows):
        n_p_tiles = t_p // tile_rows
        per_seq = dseq // tile_rows
        return lambda i: jnp.where(i < n_p_tiles, 0, 1 + (i - n_p_tiles) // per_seq)

    n_cond = -(-(1 + n_s) // 8) * 8
    cond = jnp.zeros((n_cond, d), F32).at[0].set(c_ctx).at[1:1 + n_s].set(c)
    mod = _modulation(cond, w_ada, b_ada)

    lbc = _lower_bound_consts(lb_raw)
    rope_tabs = _rope_tables(dseq)
    bias_tabs = _bias_tables(rpb)
    w_in_b = w_in.astype(BF16)
    w_out_b = w_out.astype(BF16)
    w_sh_in_b = w_sh_in.astype(BF16)
    w_sh_out_b = w_sh_out.astype(BF16)
    wr_t = jnp.swapaxes(w_router, 1, 2).astype(BF16)

    x = jnp.concatenate([x_prompt.reshape(t_p, d), x_sample.reshape(t_s, d)], axis=0)
    new_k, new_v, new_s = [], [], []
    for l in range(depth):
        proj = _inproj(x, mod[l], w_in_b[l], tm_in, cond_of_tile(tm_in))
        nrm = hgrn_norm[l].reshape(1, HEAD_DIM)
        or_p, st_p = _hgrn(proj, lbc[l], nrm, seq, n_p, 0, emit_state=True)
        (or_s,) = _hgrn(proj, lbc[l], nrm, dseq, n_s, t_p // dseq, state_in=state_hgrn, layer=l)
        oa_p, k_l, v_l = _ctx_attention(proj, seq, n_p)
        oa_s = _nbr_attention(proj, cache_k, cache_v, bias_tabs[l], rope_tabs, l, dseq, n_s, t_p // dseq)
        x1, h2, w_t, sel_t, nb = _outproj(or_p, or_s, oa_p, oa_s, x, mod[l], w_out_b[l], ln_g[l], ln_b[l], wr_t[l],
                                          router_bias[l].reshape(N_EXPERTS, 1), alpha, cond_of_tile(2 * MOE_TILE))
        used_rows, bstart, bsrc = _block_lists(nb[:, 0, :].astype(jnp.int32))
        xs = _dispatch(used_rows, h2, sel_t, w_t)
        ys = _experts(bstart, bsrc, xs, w_e_in, w_e_out, l)
        x = _combine(used_rows, ys, sel_t, h2, x1, mod[l], w_sh_in_b[l], w_sh_out_b[l], ln_g[l], ln_b[l], alpha,
                     cond_of_tile(MOE_TILE))
        new_k.append(k_l)
        new_v.append(v_l)
        new_s.append(st_p)
    return (x[:t_p].reshape(n_p, seq, d), x[t_p:].reshape(n_s, dseq, d),
            jnp.stack(new_k, axis=1), jnp.stack(new_v, axis=1), jnp.stack(new_s, axis=1))
```

```python
import functools

import jax
import jax.numpy as jnp
from jax import lax
from jax.experimental import pallas as pl
from jax.experimental.pallas import tpu as pltpu

F32 = jnp.float32
BF16 = jnp.bfloat16

D_MODEL = 2048
HEAD_DIM = 128
N_HEADS = D_MODEL // HEAD_DIM
H_R = N_HEADS // 2
H_A = N_HEADS - H_R
W_R = H_R * HEAD_DIM
W_A = H_A * HEAD_DIM
N_COLS = 5 * W_R + 3 * W_A
N_COLBLK = N_COLS // HEAD_DIM
CHUNK = 64
SUB = 16
GRID_W = 64
WIN_R = 8
WIN_C = 16
ROPE_THETA = 10000.0
N_EXPERTS = 64
TOP_K = 8
N_GROUPS = 8
GROUP_SIZE = N_EXPERTS // N_GROUPS
TOPK_GROUPS = 4
D_EXPERT = 512
D_SHARED = 512
ROUTED_SCALE = 2.5
EPS = 1e-6
LB_MAX = 1.0 - 1e-4
NEG_BIG = -1e30
EXP_CLAMP = 60.0

MOE_TILE = 256
ROW_BLK = 16
TILE_ROWS = MOE_TILE * TOP_K + N_EXPERTS * ROW_BLK
TILE_BLKS = TILE_ROWS // ROW_BLK
W_LANES = 256
XS_COLS = D_MODEL + W_LANES
DISP_ROWS = 512
COMB_SPLIT = 2
EXP_GROUP = 32
VMEM_LIMIT = 52 * 1024 * 1024


def _cparams(sem):
    return pltpu.CompilerParams(dimension_semantics=sem, vmem_limit_bytes=VMEM_LIMIT)


def _dot(a, b):
    return jnp.dot(a, b, preferred_element_type=F32)


def _dot_nt(a, b):
    return lax.dot_general(a, b, (((1,), (1,)), ((), ())), preferred_element_type=F32)


def _dot_tn(a, b):
    return lax.dot_general(a, b, (((0,), (0,)), ((), ())), preferred_element_type=F32)


def _silu(x):
    return x * jax.nn.sigmoid(x)


def _layer_norm(x):
    mu = jnp.mean(x, axis=-1, keepdims=True)
    xc = x - mu
    var = jnp.mean(xc * xc, axis=-1, keepdims=True)
    return xc * lax.rsqrt(var + EPS)


def _split3(x):
    a = x.astype(BF16)
    r = x - a.astype(F32)
    b = r.astype(BF16)
    c = (r - b.astype(F32)).astype(BF16)
    return a, b, c


def _mod_kernel(c_ref, w_ref, b_ref, o_ref):
    a = _silu(c_ref[...]).astype(BF16)
    o_ref[...] = _dot(a, w_ref[...].astype(BF16)) + b_ref[...]


def _modulation(cond, w_ada, b_ada):
    depth, d, n6 = w_ada.shape
    nc = cond.shape[0]
    tn = 1024
    return pl.pallas_call(
        _mod_kernel,
        grid=(depth, n6 // tn),
        in_specs=[
            pl.BlockSpec((nc, d), lambda l, j: (0, 0)),
            pl.BlockSpec((None, d, tn), lambda l, j: (l, 0, j)),
            pl.BlockSpec((None, 1, tn), lambda l, j: (l, 0, j)),
        ],
        out_specs=pl.BlockSpec((None, nc, tn), lambda l, j: (l, 0, j)),
        out_shape=jax.ShapeDtypeStruct((depth, nc, n6), F32),
        compiler_params=_cparams(("arbitrary", "arbitrary")),
        name="modulation",
    )(cond, w_ada, b_ada.reshape(depth, 1, n6))


def _inproj_kernel(x_ref, mod_ref, w_ref, o_ref, h_scr):
    @pl.when(pl.program_id(1) == 0)
    def _():
        y = _layer_norm(x_ref[...])
        sh = mod_ref[:, 0:D_MODEL]
        sc = mod_ref[:, D_MODEL:2 * D_MODEL]
        h_scr[...] = (y * (1.0 + sc) + sh).astype(BF16)

    res = _dot(h_scr[...], w_ref[...]).astype(BF16)
    for j in range(o_ref.shape[0]):
        o_ref[j] = res[:, j * HEAD_DIM:(j + 1) * HEAD_DIM]


def _inproj(x, mod_l, w_in_l, tm, cond_of_tile):
    t = x.shape[0]
    tn = 1024
    nblk = tn // HEAD_DIM
    nc = mod_l.shape[0]
    return pl.pallas_call(
        _inproj_kernel,
        grid=(t // tm, N_COLS // tn),
        in_specs=[
            pl.BlockSpec((tm, D_MODEL), lambda i, j: (i, 0)),
            pl.BlockSpec((None, 1, 6 * D_MODEL), lambda i, j: (cond_of_tile(i), 0, 0)),
            pl.BlockSpec((D_MODEL, tn), lambda i, j: (0, j)),
        ],
        out_specs=pl.BlockSpec((nblk, tm, HEAD_DIM), lambda i, j: (j, i, 0)),
        out_shape=jax.ShapeDtypeStruct((N_COLBLK, t, HEAD_DIM), BF16),
        scratch_shapes=[pltpu.VMEM((tm, D_MODEL), BF16)],
        compiler_params=_cparams(("arbitrary", "arbitrary")),
        name="inproj",
    )(x, mod_l.reshape(nc, 1, 6 * D_MODEL), w_in_l)


def _gates(z, log_lb, log1m_lb, one_m_lb):
    lse = jnp.log(1.0 + jnp.exp(-jnp.abs(z)))
    log_sig = jnp.minimum(z, 0.0) - lse
    k = one_m_lb * jnp.exp(log_sig - z)
    c = log1m_lb + log_sig
    g = jnp.maximum(log_lb, c) + jnp.log(1.0 + jnp.exp(-jnp.abs(log_lb - c)))
    return k, g


def _chunk_cumsum(tri, g):
    r = _dot(tri, jnp.concatenate(_split3(g), axis=1))
    return r[:, 0:HEAD_DIM] + r[:, HEAD_DIM:2 * HEAD_DIM] + r[:, 2 * HEAD_DIM:3 * HEAD_DIM]


def _intra_scores(q, k, b, reverse):
    zeros = jnp.zeros((SUB, HEAD_DIM), F32)
    nsub = CHUNK // SUB
    lhs_rows, rhs_slots = [], []
    for blk in range(nsub):
        lo = blk * SUB
        mid = lo + SUB // 2 if reverse else lo + SUB // 2 - 1
        r = b[mid:mid + 1, :]
        qe = q[lo:lo + SUB] * jnp.exp(jnp.minimum(b[lo:lo + SUB] - r, EXP_CLAMP))
        lhs_rows.append(jnp.concatenate([qe if j == blk else zeros for j in range(nsub)], axis=1))
        rhs_slots.append(k * jnp.exp(jnp.minimum(r - b, EXP_CLAMP)))
    lhs = jnp.concatenate(lhs_rows, axis=0).astype(BF16)
    rhs = jnp.concatenate(rhs_slots, axis=1).astype(BF16)
    return _dot_nt(lhs, rhs)


def _intra_scores_exact(q, k, b, reverse):
    s_idx = lax.broadcasted_iota(jnp.int32, (CHUNK, HEAD_DIM), 0)
    r_idx = lax.broadcasted_iota(jnp.int32, (SUB, HEAD_DIM), 0)
    zeros = jnp.zeros((SUB, HEAD_DIM), F32)
    out = []
    for blk in range(CHUNK // SUB):
        lo = blk * SUB
        qb = q[lo:lo + SUB]
        lhs_slots, rhs_slots = [], []
        for j in range(SUB):
            t = lo + j
            valid = (s_idx >= t) if reverse else (s_idx <= t)
            rhs_slots.append(k * jnp.exp(jnp.where(valid, b[t:t + 1, :] - b, NEG_BIG)))
            lhs_slots.append(jnp.where(r_idx == j, qb, zeros))
        lhs = jnp.concatenate(lhs_slots, axis=1).astype(BF16)
        rhs = jnp.concatenate(rhs_slots, axis=1).astype(BF16)
        out.append(_dot_nt(lhs, rhs))
    return jnp.concatenate(out, axis=0)


def _hgrn_kernel(*refs, seq_len, has_init, emit_state):
    q_ref, i_ref, zf_ref, zb_ref, g_ref, lbc_ref, nrm_ref = refs[:7]
    pos = 7
    s0_ref = None
    if has_init:
        s0_ref = refs[pos]
        pos += 1
    o_ref = refs[pos]
    pos += 1
    so_ref = None
    if emit_state:
        so_ref = refs[pos]
        pos += 1
    o_scr, qd_scr, ut_scr, dec_scr = refs[pos:pos + 4]

    n_chunks = seq_len // CHUNK
    ri = lax.broadcasted_iota(jnp.int32, (CHUNK, CHUNK), 0)
    ci = lax.broadcasted_iota(jnp.int32, (CHUNK, CHUNK), 1)
    tri_f = jnp.where(ci <= ri, 1.0, 0.0).astype(BF16)
    tri_b = jnp.where(ci >= ri, 1.0, 0.0).astype(BF16)
    lbc = lbc_ref[...]
    lbc_f = (lbc[0:1], lbc[1:2], lbc[2:3])
    lbc_b = (lbc[3:4], lbc[4:5], lbc[5:6])

    if has_init:
        st_f0 = s0_ref[0].T
        st_b0 = s0_ref[1].T
    else:
        st_f0 = jnp.zeros((HEAD_DIM, HEAD_DIM), F32)
        st_b0 = jnp.zeros((HEAD_DIM, HEAD_DIM), F32)

    per_it = 4
    dirs = ((zf_ref, lbc_f, tri_f, ci <= ri, False), (zb_ref, lbc_b, tri_b, ci >= ri, True))

    def decay_floor(z_ref, lbc_d):
        zneg = jnp.minimum(z_ref[...].astype(F32), 0.0)
        return jnp.min(jnp.maximum(lbc_d[0], lbc_d[1] + zneg - 0.6931472))

    safe = jnp.minimum(decay_floor(zf_ref, lbc_f), decay_floor(zb_ref, lbc_b)) >= -EXP_CLAMP / (SUB // 2)

    def local(it, carry, scores):
        chains = []
        for u in range(per_it):
            c = it * per_it + u
            rows = pl.ds(pl.multiple_of(c * CHUNK, CHUNK), CHUNK)
            q = _silu(q_ref[rows, :].astype(F32))
            v = i_ref[rows, :]
            for d, (z_ref, lbc_d, tri, keep, rev) in enumerate(dirs):
                k, g = _gates(z_ref[rows, :].astype(F32), lbc_d[0], lbc_d[1], lbc_d[2])
                chains.append(dict(c=c, rows=rows, d=d, q=q, v=v, k=k, g=g, tri=tri, keep=keep, rev=rev))
        for ch in chains:
            ch["b"] = _chunk_cumsum(ch["tri"], ch["g"])
        for ch in chains:
            ch["a"] = scores(ch["q"], ch["k"], ch["b"], ch["rev"])
        for ch in chains:
            d, rows, b = ch["d"], ch["rows"], ch["b"]
            a = jnp.where(ch["keep"], ch["a"], 0.0).astype(BF16)
            o_scr[d, rows, :] = _dot(a, ch["v"])
            qd_scr[d, rows, :] = (ch["q"] * jnp.exp(b)).astype(BF16)
            edge = b[0:1, :] if ch["rev"] else b[CHUNK - 1:CHUNK, :]
            ut_scr[d, ch["c"]] = _dot_tn(ch["v"], (ch["k"] * jnp.exp(edge - b)).astype(BF16))
            dec_scr[d, ch["c"]] = jnp.exp(edge)
        return carry

    @pl.when(safe)
    def _():
        lax.fori_loop(0, n_chunks // per_it, functools.partial(local, scores=_intra_scores), 0)

    @pl.when(jnp.logical_not(safe))
    def _():
        lax.fori_loop(0, n_chunks // per_it, functools.partial(local, scores=_intra_scores_exact), 0)

    def scan(j, carry):
        new = []
        for d, st in enumerate(carry):
            c = j if d == 0 else n_chunks - 1 - j
            upd = ut_scr[d, c]
            ut_scr[d, c] = st
            new.append(st * dec_scr[d, c] + upd)
        return tuple(new)

    st_f, st_b = lax.fori_loop(0, n_chunks, scan, (st_f0, st_b0), unroll=2)

    per_it3 = 4

    def inter(it, carry):
        prods = []
        for u in range(per_it3):
            c = it * per_it3 + u
            rows = pl.ds(pl.multiple_of(c * CHUNK, CHUNK), CHUNK)
            for d in range(2):
                prods.append((d, rows, _dot_nt(qd_scr[d, rows, :], ut_scr[d, c].astype(BF16))))
        for d, rows, p in prods:
            o_scr[d, rows, :] += p
        return carry

    lax.fori_loop(0, n_chunks // per_it3, inter, 0)
    if emit_state:
        so_ref[0] = st_f.T
        so_ref[1] = st_b.T

    nrm = nrm_ref[...]
    piece = 256 if seq_len % 256 == 0 else CHUNK

    def fin(p, carry):
        r0 = pl.multiple_of(p * piece, piece)
        o = o_scr[0, pl.ds(r0, piece), :] + o_scr[1, pl.ds(r0, piece), :]
        o = o * lax.rsqrt(jnp.mean(o * o, axis=-1, keepdims=True) + EPS) * nrm
        gate = g_ref[pl.ds(r0, piece), :].astype(F32)
        o_ref[pl.ds(r0, piece), :] = (o * _silu(gate)).astype(BF16)
        return carry

    lax.fori_loop(0, seq_len // piece, fin, 0)


def _hgrn(proj, lbc, nrm, seq_len, n_seq, row_blk0, state_in=None, layer=0, emit_state=False):
    kern = functools.partial(_hgrn_kernel, seq_len=seq_len, has_init=state_in is not None, emit_state=emit_state)

    def col(sec):
        return pl.BlockSpec((None, seq_len, HEAD_DIM), lambda b, h: (sec * H_R + h, row_blk0 + b, 0))

    in_specs = [col(0), col(1), col(2), col(3), col(4),
                pl.BlockSpec((8, HEAD_DIM), lambda b, h: (0, h)),
                pl.BlockSpec((1, HEAD_DIM), lambda b, h: (0, 0))]
    args = [proj, proj, proj, proj, proj, lbc, nrm]
    if state_in is not None:
        in_specs.append(pl.BlockSpec((None, None, 2, None, HEAD_DIM, HEAD_DIM),
                                     lambda b, h: (b, layer, 0, h, 0, 0)))
        args.append(state_in)
    out_specs = [pl.BlockSpec((seq_len, HEAD_DIM), lambda b, h: (b, h))]
    out_shape = [jax.ShapeDtypeStruct((n_seq * seq_len, W_R), BF16)]
    if emit_state:
        out_specs.append(pl.BlockSpec((None, 2, None, HEAD_DIM, HEAD_DIM), lambda b, h: (b, 0, h, 0, 0)))
        out_shape.append(jax.ShapeDtypeStruct((n_seq, 2, H_R, HEAD_DIM, HEAD_DIM), F32))
    res = pl.pallas_call(
        kern,
        grid=(n_seq, H_R),
        in_specs=in_specs,
        out_specs=out_specs,
        out_shape=out_shape,
        scratch_shapes=[pltpu.VMEM((2, seq_len, HEAD_DIM), F32),
                        pltpu.VMEM((2, seq_len, HEAD_DIM), BF16),
                        pltpu.VMEM((2, seq_len // CHUNK, HEAD_DIM, HEAD_DIM), F32),
                        pltpu.VMEM((2, seq_len // CHUNK, 1, HEAD_DIM), F32)],
        compiler_params=_cparams(("arbitrary", "arbitrary")),
        name="hgrn_state" if emit_state else "hgrn",
    )(*args)
    return res


def _ctx_attn_kernel(q_ref, k_ref, v_ref, o_ref, ko_ref, vo_ref):
    heads = range(H_A)
    scores = [_dot_nt(q_ref[h], k_ref[h]) * (HEAD_DIM ** -0.5) for h in heads]
    probs, dens = [], []
    for s in scores:
        p = jnp.exp(s - jnp.max(s, axis=-1, keepdims=True))
        dens.append(jnp.sum(p, axis=-1, keepdims=True))
        probs.append(p.astype(BF16))
    for h in heads:
        o = _dot(probs[h], v_ref[h]) / dens[h]
        o_ref[:, h * HEAD_DIM:(h + 1) * HEAD_DIM] = o.astype(BF16)
        ko_ref[h] = k_ref[h].astype(F32)
        vo_ref[h] = v_ref[h].astype(F32)


def _ctx_attention(proj, seq_len, n_seq):
    def col(sec):
        return pl.BlockSpec((H_A, seq_len, HEAD_DIM), lambda b: ((5 * H_R + sec * H_A) // H_A, b, 0))

    cache_spec = pl.BlockSpec((None, H_A, seq_len, HEAD_DIM), lambda b: (b, 0, 0, 0))
    cache_shape = jax.ShapeDtypeStruct((n_seq, H_A, seq_len, HEAD_DIM), F32)
    return pl.pallas_call(
        _ctx_attn_kernel,
        grid=(n_seq,),
        in_specs=[col(0), col(1), col(2)],
        out_specs=[pl.BlockSpec((seq_len, W_A), lambda b: (b, 0)), cache_spec, cache_spec],
        out_shape=[jax.ShapeDtypeStruct((n_seq * seq_len, W_A), BF16), cache_shape, cache_shape],
        compiler_params=_cparams(("arbitrary",)),
        name="ctx_attention",
    )(proj, proj, proj)


def _rope(x, cos, sin_lo, sin_hi):
    return (x * cos + pltpu.roll(x, HEAD_DIM - HEAD_DIM // 4, axis=1) * sin_lo
            + pltpu.roll(x, HEAD_DIM // 4, axis=1) * sin_hi)


def _nbr_attn_kernel(q_ref, k_ref, v_ref, kc_ref, vc_ref, bias_ref, cos_ref, slo_ref, shi_ref, o_ref, kr_scr,
                     *, n_rows):
    rows_pc = 256

    def rope_k(p, carry):
        r0 = pl.multiple_of(p * rows_pc, rows_pc)
        sl = pl.ds(r0, rows_pc)
        kr_scr[sl, :] = _rope(k_ref[sl, :].astype(F32), cos_ref[sl, :], slo_ref[sl, :], shi_ref[sl, :]).astype(BF16)
        return carry

    lax.fori_loop(0, (n_rows * GRID_W) // rows_pc, rope_k, 0)
    kc = kc_ref[...].astype(BF16)
    vc = vc_ref[...].astype(BF16)
    scale = HEAD_DIM ** -0.5
    win = WIN_R * GRID_W

    per_it = 8 if n_rows % 8 == 0 else 1

    def rows_step(it, carry):
        items = []
        for u in range(per_it):
            r = it * per_it + u
            rs = jnp.clip(r - WIN_R // 2, 0, n_rows - WIN_R)
            qs = pl.ds(pl.multiple_of(r * GRID_W, GRID_W), GRID_W)
            ks = pl.ds(pl.multiple_of(rs * GRID_W, GRID_W), win)
            q = _rope(q_ref[qs, :].astype(F32), cos_ref[qs, :], slo_ref[qs, :], shi_ref[qs, :])
            items.append(dict(qs=qs, ks=ks, dl=rs - r + WIN_R - 1, q=(q * scale).astype(BF16)))
        for x in items:
            x["s_loc"] = _dot_nt(x["q"], kr_scr[x["ks"], :]) + bias_ref[x["dl"]]
            x["s_ctx"] = _dot_nt(x["q"], kc)
        for x in items:
            m = jnp.maximum(jnp.max(x["s_loc"], axis=-1, keepdims=True), jnp.max(x["s_ctx"], axis=-1, keepdims=True))
            p_loc = jnp.exp(x["s_loc"] - m)
            p_ctx = jnp.exp(x["s_ctx"] - m)
            x["den"] = jnp.sum(p_loc, axis=-1, keepdims=True) + jnp.sum(p_ctx, axis=-1, keepdims=True)
            x["p_loc"] = p_loc.astype(BF16)
            x["p_ctx"] = p_ctx.astype(BF16)
        for x in items:
            o = _dot(x["p_loc"], v_ref[x["ks"], :]) + _dot(x["p_ctx"], vc)
            o_ref[x["qs"], :] = (o / x["den"]).astype(BF16)
        return carry

    lax.fori_loop(0, n_rows // per_it, rows_step, 0)


def _nbr_attention(proj, cache_k, cache_v, bias_tab, rope_tabs, layer, seq_len, n_seq, row_blk0):
    n_rows = seq_len // GRID_W
    past = cache_k.shape[3]
    kern = functools.partial(_nbr_attn_kernel, n_rows=n_rows)

    def col(sec):
        return pl.BlockSpec((None, seq_len, HEAD_DIM), lambda b, h: (5 * H_R + sec * H_A + h, row_blk0 + b, 0))

    cache_spec = pl.BlockSpec((None, None, None, past, HEAD_DIM), lambda b, h: (b, layer, h, 0, 0))
    tab_spec = pl.BlockSpec((seq_len, HEAD_DIM), lambda b, h: (0, 0))
    return pl.pallas_call(
        kern,
        grid=(n_seq, H_A),
        in_specs=[col(0), col(1), col(2), cache_spec, cache_spec,
                  pl.BlockSpec((None, WIN_R, GRID_W, WIN_R * GRID_W), lambda b, h: (h, 0, 0, 0)),
                  tab_spec, tab_spec, tab_spec],
        out_specs=pl.BlockSpec((seq_len, HEAD_DIM), lambda b, h: (b, h)),
        out_shape=jax.ShapeDtypeStruct((n_seq * seq_len, W_A), BF16),
        scratch_shapes=[pltpu.VMEM((seq_len, HEAD_DIM), BF16)],
        compiler_params=_cparams(("arbitrary", "arbitrary")),
        name="nbr_attention",
    )(proj, proj, proj, cache_k, cache_v, bias_tab, *rope_tabs)


def _first_argmax_rows(vals, ids, sentinel):
    m = vals[0]
    for v in vals[1:]:
        m = jnp.maximum(m, v)
    m = jnp.max(m, axis=0, keepdims=True)
    best = None
    for v, i in zip(vals, ids):
        cand = jnp.min(jnp.where(v == m, i, sentinel), axis=0, keepdims=True)
        best = cand if best is None else jnp.minimum(best, cand)
    return m, best


def _route(s, sb):
    n = s.shape[1]
    iota8 = lax.broadcasted_iota(jnp.int32, (GROUP_SIZE, n), 0)
    neg_inf = -jnp.inf
    groups = [sb[g * GROUP_SIZE:(g + 1) * GROUP_SIZE, :] for g in range(N_GROUPS)]
    gscore = jnp.zeros((N_GROUPS, n), F32)
    for g, xg in enumerate(groups):
        m1, i1 = _first_argmax_rows([xg], [iota8], GROUP_SIZE)
        m2 = jnp.max(jnp.where(iota8 == i1, neg_inf, xg), axis=0, keepdims=True)
        gscore = jnp.where(iota8 == g, m1 + m2, gscore)
    gsel = jnp.zeros((N_GROUPS, n), F32)
    cur = gscore
    for _ in range(TOPK_GROUPS):
        _, i = _first_argmax_rows([cur], [iota8], N_GROUPS)
        hit = iota8 == i
        gsel = jnp.where(hit, 1.0, gsel)
        cur = jnp.where(hit, neg_inf, cur)
    cur = [jnp.where(gsel[g:g + 1, :] > 0.5, xg, neg_inf) for g, xg in enumerate(groups)]
    ids = [iota8 + g * GROUP_SIZE for g in range(N_GROUPS)]
    sel = [jnp.zeros((GROUP_SIZE, n), jnp.bool_) for _ in range(N_GROUPS)]
    for _ in range(TOP_K):
        _, i = _first_argmax_rows(cur, ids, N_EXPERTS)
        hits = [idg == i for idg in ids]
        sel = [jnp.logical_or(a, h) for a, h in zip(sel, hits)]
        cur = [jnp.where(h, neg_inf, c) for c, h in zip(cur, hits)]
    return sel


def _outproj_kernel(orp_ref, ors_ref, oap_ref, oas_ref, x_ref, mod_ref, w_ref, lng_ref, lnb_ref, wr_ref, rb_ref,
                    x1_ref, h2_ref, wt_ref, sel_ref, nb_ref, *, alpha, n_prompt_tiles):
    from_prompt = pl.program_id(0) < n_prompt_tiles
    n = MOE_TILE
    tiles = [slice(i * n, (i + 1) * n) for i in range(x_ref.shape[0] // n)]
    mixes = []
    for sl in tiles:
        o_r = jnp.where(from_prompt, orp_ref[sl, :], ors_ref[sl, :])
        o_a = jnp.where(from_prompt, oap_ref[sl, :], oas_ref[sl, :])
        mixes.append(_dot(o_r, w_ref[0:W_R, :]) + _dot(o_a, w_ref[W_R:W_R + W_A, :]))
    g1 = mod_ref[:, 2 * D_MODEL:3 * D_MODEL]
    sh2 = mod_ref[:, 3 * D_MODEL:4 * D_MODEL]
    sc2 = mod_ref[:, 4 * D_MODEL:5 * D_MODEL]
    scores = []
    for sl, mix in zip(tiles, mixes):
        x1 = _layer_norm(alpha * x_ref[sl, :] + g1 * mix) * lng_ref[0:1, :] + lnb_ref[0:1, :]
        x1_ref[sl, :] = x1
        h2 = (_layer_norm(x1) * (1.0 + sc2) + sh2).astype(BF16)
        h2_ref[sl, :] = h2
        scores.append(jax.nn.sigmoid(_dot_nt(wr_ref[...], h2)))
    for i, (sl, s) in enumerate(zip(tiles, scores)):
        sel = _route(s, s + rb_ref[...])
        wsel = [jnp.where(sel[g], s[g * GROUP_SIZE:(g + 1) * GROUP_SIZE, :], 0.0) for g in range(N_GROUPS)]
        tot = wsel[0]
        for w in wsel[1:]:
            tot = tot + w
        tot = jnp.sum(tot, axis=0, keepdims=True)
        for g in range(N_GROUPS):
            rows = slice(g * GROUP_SIZE, (g + 1) * GROUP_SIZE)
            wt_ref[rows, sl] = wsel[g] / tot * ROUTED_SCALE
            sel_ref[rows, sl] = jnp.where(sel[g], 1.0, 0.0)
        cnt = _dot_nt(jnp.ones((8, n), BF16), sel_ref[:, sl].astype(BF16))
        nb_ref[i] = jnp.floor((cnt + (ROW_BLK - 1)) * (1.0 / ROW_BLK))


def _outproj(or_p, or_s, oa_p, oa_s, x, mod_l, w_out_l, ln_g_l, ln_b_l, wr_t, rbias, alpha, cond_of_tile):
    t = x.shape[0]
    per_step = 2
    tm = per_step * MOE_TILE
    nt = t // tm
    n_p = or_p.shape[0] // tm
    nc = mod_l.shape[0]
    kern = functools.partial(_outproj_kernel, alpha=alpha, n_prompt_tiles=n_p)
    row = lambda i: (i, 0)
    const = lambda i: (0, 0)
    prompt_row = lambda i: (jnp.minimum(i, n_p - 1), 0)
    sample_row = lambda i: (jnp.maximum(i - n_p, 0), 0)
    return pl.pallas_call(
        kern,
        grid=(nt,),
        in_specs=[
            pl.BlockSpec((tm, W_R), prompt_row),
            pl.BlockSpec((tm, W_R), sample_row),
            pl.BlockSpec((tm, W_A), prompt_row),
            pl.BlockSpec((tm, W_A), sample_row),
            pl.BlockSpec((tm, D_MODEL), row),
            pl.BlockSpec((None, 1, 6 * D_MODEL), lambda i: (cond_of_tile(i), 0, 0)),
            pl.BlockSpec((W_R + W_A, D_MODEL), const),
            pl.BlockSpec((2, D_MODEL), const),
            pl.BlockSpec((2, D_MODEL), const),
            pl.BlockSpec((N_EXPERTS, D_MODEL), const),
            pl.BlockSpec((N_EXPERTS, 1), const),
        ],
        out_specs=[
            pl.BlockSpec((tm, D_MODEL), row),
            pl.BlockSpec((tm, D_MODEL), row),
            pl.BlockSpec((N_EXPERTS, tm), lambda i: (0, i)),
            pl.BlockSpec((N_EXPERTS, tm), lambda i: (0, i)),
            pl.BlockSpec((per_step, 8, N_EXPERTS), lambda i: (i, 0, 0)),
        ],
        out_shape=[
            jax.ShapeDtypeStruct((t, D_MODEL), F32),
            jax.ShapeDtypeStruct((t, D_MODEL), BF16),
            jax.ShapeDtypeStruct((N_EXPERTS, t), F32),
            jax.ShapeDtypeStruct((N_EXPERTS, t), F32),
            jax.ShapeDtypeStruct((nt * per_step, 8, N_EXPERTS), F32),
        ],
        compiler_params=_cparams(("arbitrary",)),
        name="outproj_router",
    )(or_p, or_s, oa_p, oa_s, x, mod_l.reshape(nc, 1, 6 * D_MODEL), w_out_l, ln_g_l, ln_b_l, wr_t, rbias)


def _sorted_layout(sel):
    n = sel.shape[1]
    selb = sel.astype(BF16)
    ti = lax.broadcasted_iota(jnp.int32, (n, n), 0)
    tj = lax.broadcasted_iota(jnp.int32, (n, n), 1)
    rank = _dot(selb, jnp.where(ti < tj, 1.0, 0.0).astype(BF16))
    cnt = _dot(selb, jnp.ones((n, n), BF16))
    nb = jnp.floor((cnt + (ROW_BLK - 1)) * (1.0 / ROW_BLK))
    ei = lax.broadcasted_iota(jnp.int32, (N_EXPERTS, N_EXPERTS), 0)
    ej = lax.broadcasted_iota(jnp.int32, (N_EXPERTS, N_EXPERTS), 1)
    boff = _dot(jnp.where(ej < ei, 1.0, 0.0).astype(BF16), nb.astype(BF16))
    eye = lax.broadcasted_iota(jnp.int32, (N_EXPERTS, n), 0) == lax.broadcasted_iota(jnp.int32, (N_EXPERTS, n), 1)
    ones8 = jnp.ones((8, N_EXPERTS), BF16)
    boff_row = _dot(ones8, jnp.where(eye, boff, 0.0).astype(BF16))[0:1, 0:N_EXPERTS]
    nb_row = _dot(ones8, jnp.where(eye, nb, 0.0).astype(BF16))[0:1, 0:N_EXPERTS]
    rankp = jnp.where(sel > 0.5, rank + 1.0, 0.0)
    return rankp, boff, boff_row, nb_row


def _row_expert_onehot(row0, n_rows, boff_row, nb_row):
    blk = (lax.broadcasted_iota(jnp.int32, (n_rows, N_EXPERTS), 0) + row0) // ROW_BLK
    blk = blk.astype(F32)
    return jnp.where(jnp.logical_and(blk >= boff_row, blk < boff_row + nb_row), 1.0, 0.0).astype(BF16)


def _sort_onehots(chunks, keys, boff_row, nb_row):
    n = keys.shape[1] // 2
    onehots = [_row_expert_onehot(c * DISP_ROWS, DISP_ROWS, boff_row, nb_row) for c in chunks]
    looked = [_dot(e, keys) for e in onehots]
    out = []
    for c, qk in zip(chunks, looked):
        row = lax.broadcasted_iota(jnp.int32, (DISP_ROWS, n), 0) + (c * DISP_ROWS + 1)
        out.append(jnp.where(qk[:, 0:n] == row.astype(F32) - ROW_BLK * qk[:, n:2 * n], 1.0, 0.0).astype(BF16))
    return out


def _dispatch_kernel(used_ref, h2_ref, sel_ref, wt_ref, xs_ref, xa_scr):
    s = pl.program_id(0)
    n = h2_ref.shape[0]
    rankp, boff, boff_row, nb_row = _sorted_layout(sel_ref[...])
    keys = jnp.concatenate([rankp, boff], axis=1).astype(BF16)
    ti = lax.broadcasted_iota(jnp.int32, (n, n), 0)
    tj = lax.broadcasted_iota(jnp.int32, (n, n), 1)
    eye = jnp.where(ti == tj, 1.0, 0.0).astype(BF16)
    w1, w2, w3 = _split3(wt_ref[...])
    xa_scr[:, 0:D_MODEL] = h2_ref[...]
    xa_scr[:, D_MODEL:D_MODEL + 128] = _dot_nt(eye, jnp.concatenate([w1, w2], axis=0)).astype(BF16)
    xa_scr[:, D_MODEL + 128:XS_COLS] = _dot_nt(eye, jnp.concatenate([w3, jnp.zeros_like(w3)], axis=0)).astype(BF16)

    def emit(chunks):
        for c, g in zip(chunks, _sort_onehots(chunks, keys, boff_row, nb_row)):
            xs_ref[c * DISP_ROWS:(c + 1) * DISP_ROWS, :] = _dot(g, xa_scr[...]).astype(BF16)

    always = (MOE_TILE * TOP_K) // DISP_ROWS
    emit(tuple(range(always)))
    for c in range(always, TILE_ROWS // DISP_ROWS):
        @pl.when(c * DISP_ROWS < used_ref[s])
        def _():
            emit((c,))

        @pl.when(c * DISP_ROWS >= used_ref[s])
        def _():
            xs_ref[c * DISP_ROWS:(c + 1) * DISP_ROWS, :] = jnp.zeros((DISP_ROWS, XS_COLS), BF16)


def _dispatch(used_rows, h2, sel_t, w_t):
    t = h2.shape[0]
    nt = t // MOE_TILE
    return pl.pallas_call(
        _dispatch_kernel,
        grid_spec=pltpu.PrefetchScalarGridSpec(
            num_scalar_prefetch=1,
            grid=(nt,),
            in_specs=[
                pl.BlockSpec((MOE_TILE, D_MODEL), lambda s, u: (s, 0)),
                pl.BlockSpec((N_EXPERTS, MOE_TILE), lambda s, u: (0, s)),
                pl.BlockSpec((N_EXPERTS, MOE_TILE), lambda s, u: (0, s)),
            ],
            out_specs=pl.BlockSpec((TILE_ROWS, XS_COLS), lambda s, u: (s, 0)),
            scratch_shapes=[pltpu.VMEM((MOE_TILE, XS_COLS), BF16)],
        ),
        out_shape=jax.ShapeDtypeStruct((nt * TILE_ROWS, XS_COLS), BF16),
        compiler_params=_cparams(("arbitrary",)),
        name="moe_dispatch",
    )(used_rows, h2, sel_t, w_t)


def _expert_kernel(bstart_ref, bsrc_ref, xs_hbm, w1_ref, w2_ref, ys_hbm, w1_scr, w2_scr, x_buf, y_buf, st_ref,
                   sem_in, sem_out):
    e = pl.program_id(0)
    b0 = bstart_ref[e]
    n_blk = bstart_ref[e + 1] - b0
    n_groups = (n_blk + EXP_GROUP - 1) // EXP_GROUP

    def in_copy(j, slot, k):
        src = pl.multiple_of(bsrc_ref[j] * ROW_BLK, ROW_BLK)
        row = k * ROW_BLK if isinstance(k, int) else pl.multiple_of(k * ROW_BLK, ROW_BLK)
        return pltpu.make_async_copy(xs_hbm.at[pl.ds(src, ROW_BLK), :],
                                     x_buf.at[slot, pl.ds(row, ROW_BLK), :], sem_in.at[slot])

    def out_copy(j, slot, k):
        dst = pl.multiple_of(bsrc_ref[j] * ROW_BLK, ROW_BLK)
        row = k * ROW_BLK if isinstance(k, int) else pl.multiple_of(k * ROW_BLK, ROW_BLK)
        return pltpu.make_async_copy(y_buf.at[slot, pl.ds(row, ROW_BLK), :],
                                     ys_hbm.at[pl.ds(dst, ROW_BLK), pl.ds(0, D_MODEL)], sem_out.at[slot])

    def for_blocks(pos0, cnt, fn):
        @pl.when(cnt == EXP_GROUP)
        def _():
            for k in range(EXP_GROUP):
                fn(pos0 + k, k)

        @pl.when(cnt < EXP_GROUP)
        def _():
            def body(k, carry):
                fn(pos0 + k, k)
                return carry

            lax.fori_loop(0, cnt, body, 0)

    @pl.when(e == 0)
    def _():
        x_buf[...] = jnp.zeros(x_buf.shape, BF16)
        for i in range(6):
            st_ref[i] = 0

    g0 = st_ref[0]
    issued = st_ref[5]
    st_ref[5] = 0

    @pl.when(jnp.logical_and(n_groups > 0, issued == 0))
    def _():
        for_blocks(b0, jnp.minimum(n_blk, EXP_GROUP), lambda j, k: in_copy(j, g0 % 2, k).start())

    w1_scr[...] = w1_ref[...].astype(BF16)
    w2_scr[...] = w2_ref[...].astype(BF16)

    nxt = jnp.minimum(e + 1, N_EXPERTS)
    next_b0 = bstart_ref[nxt]
    next_n = bstart_ref[jnp.minimum(e + 2, N_EXPERTS)] - next_b0

    def group(gi, carry):
        slot = (g0 + gi) % 2
        pos = b0 + gi * EXP_GROUP
        cnt = jnp.minimum(n_blk - gi * EXP_GROUP, EXP_GROUP)
        last = gi + 1 >= n_groups

        @pl.when(jnp.logical_not(last))
        def _():
            for_blocks(pos + EXP_GROUP, jnp.minimum(n_blk - (gi + 1) * EXP_GROUP, EXP_GROUP),
                       lambda j, k: in_copy(j, 1 - slot, k).start())

        @pl.when(jnp.logical_and(last, next_n > 0))
        def _():
            for_blocks(next_b0, jnp.minimum(next_n, EXP_GROUP), lambda j, k: in_copy(j, 1 - slot, k).start())
            st_ref[5] = 1

        for_blocks(pos, cnt, lambda j, k: in_copy(j, slot, k).wait())

        old_pos = st_ref[1 + 2 * slot]
        old_cnt = st_ref[2 + 2 * slot]

        @pl.when(old_cnt > 0)
        def _():
            for_blocks(old_pos, old_cnt, lambda j, k: out_copy(j, slot, k).wait())

        def ffn(rows):
            x = x_buf[slot, 0:rows, 0:D_MODEL]
            lane = lax.broadcasted_iota(jnp.int32, (rows, W_LANES), 1)
            mine = (lane & (N_EXPERTS - 1)) == e
            wrow = jnp.sum(jnp.where(mine, x_buf[slot, 0:rows, D_MODEL:XS_COLS].astype(F32), 0.0),
                           axis=1, keepdims=True)
            h = _dot(x, w1_scr[...])
            act = (_silu(h[:, 0:D_EXPERT]) * h[:, D_EXPERT:2 * D_EXPERT]).astype(BF16)
            y_buf[slot, 0:rows, :] = (_dot(act, w2_scr[...]) * wrow).astype(BF16)

        full_rows = EXP_GROUP * ROW_BLK
        short = cnt <= EXP_GROUP // 2

        @pl.when(jnp.logical_not(short))
        def _():
            ffn(full_rows)

        @pl.when(short)
        def _():
            ffn(full_rows // 2)

        for_blocks(pos, cnt, lambda j, k: out_copy(j, slot, k).start())
        st_ref[1 + 2 * slot] = pos
        st_ref[2 + 2 * slot] = cnt
        return carry

    lax.fori_loop(0, n_groups, group, 0)
    st_ref[0] = g0 + n_groups

    @pl.when(e == N_EXPERTS - 1)
    def _():
        for slot in range(2):
            @pl.when(st_ref[2 + 2 * slot] > 0)
            def _():
                for_blocks(st_ref[1 + 2 * slot], st_ref[2 + 2 * slot], lambda j, k: out_copy(j, slot, k).wait())


def _experts(bstart, bsrc, xs, w_e_in, w_e_out, layer):
    rows = EXP_GROUP * ROW_BLK
    return pl.pallas_call(
        _expert_kernel,
        grid_spec=pltpu.PrefetchScalarGridSpec(
            num_scalar_prefetch=2,
            grid=(N_EXPERTS,),
            in_specs=[
                pl.BlockSpec(memory_space=pl.ANY),
                pl.BlockSpec((None, None, D_MODEL, 2 * D_EXPERT), lambda e, bs, br: (layer, e, 0, 0)),
                pl.BlockSpec((None, None, D_EXPERT, D_MODEL), lambda e, bs, br: (layer, e, 0, 0)),
            ],
            out_specs=pl.BlockSpec(memory_space=pl.ANY),
            scratch_shapes=[
                pltpu.VMEM((D_MODEL, 2 * D_EXPERT), BF16),
                pltpu.VMEM((D_EXPERT, D_MODEL), BF16),
                pltpu.VMEM((2, rows, XS_COLS), BF16),
                pltpu.VMEM((2, rows, D_MODEL), BF16),
                pltpu.SMEM((8,), jnp.int32),
                pltpu.SemaphoreType.DMA((2,)),
                pltpu.SemaphoreType.DMA((2,)),
            ],
        ),
        out_shape=jax.ShapeDtypeStruct(xs.shape, xs.dtype),
        input_output_aliases={2: 0},
        compiler_params=_cparams(("arbitrary",)),
        name="moe_experts",
    )(bstart, bsrc, xs, w_e_in, w_e_out)


def _combine_kernel(used_ref, ys_ref, sel_ref, h2_ref, x1_ref, mod_ref, wsi_ref, wso_ref, lng_ref, lnb_ref,
                    o_ref, acc_scr, keyt_scr, row_scr, *, alpha, tile0):
    s = pl.program_id(0) + tile0
    half = pl.program_id(1)
    n = h2_ref.shape[0]
    per_step = TILE_ROWS // DISP_ROWS // COMB_SPLIT
    always = (MOE_TILE * TOP_K) // DISP_ROWS

    @pl.when(half == 0)
    def _():
        rankp, _, boff_row, nb_row = _sorted_layout(sel_ref[...])
        ti = lax.broadcasted_iota(jnp.int32, (n, n), 0)
        tj = lax.broadcasted_iota(jnp.int32, (n, n), 1)
        eye = jnp.where(ti == tj, 1.0, 0.0).astype(BF16)
        keyt_scr[0:n, :] = _dot_nt(eye, rankp.astype(BF16)).astype(BF16)
        keyt_scr[n:2 * n, :] = jnp.broadcast_to(boff_row, (n, N_EXPERTS)).astype(BF16)
        row_scr[0:1, :] = boff_row
        row_scr[1:2, :] = nb_row
        h = _dot(h2_ref[...], wsi_ref[...])
        act = (_silu(h[:, 0:D_SHARED]) * h[:, D_SHARED:2 * D_SHARED]).astype(BF16)
        acc_scr[...] = _dot(act, wso_ref[...])

    def unsort(chunks):
        onehots = [_row_expert_onehot(g * DISP_ROWS, DISP_ROWS, row_scr[0:1, :], row_scr[1:2, :]) for g in chunks]
        looked = [_dot_nt(keyt_scr[...], e) for e in onehots]
        total = None
        for g, qk in zip(chunks, looked):
            row = lax.broadcasted_iota(jnp.int32, (n, DISP_ROWS), 1) + (g * DISP_ROWS + 1)
            c = jnp.where(qk[0:n] == row.astype(F32) - ROW_BLK * qk[n:2 * n], 1.0, 0.0).astype(BF16)
            lo = (g % per_step) * DISP_ROWS
            part = _dot(c, ys_ref[lo:lo + DISP_ROWS, :])
            total = part if total is None else total + part
        acc_scr[...] += total

    for step in range(COMB_SPLIT):
        @pl.when(half == step)
        def _():
            mine = range(step * per_step, (step + 1) * per_step)
            sure = tuple(g for g in mine if g < always)
            if sure:
                unsort(sure)
            for g in mine:
                if g >= always:
                    @pl.when(g * DISP_ROWS < used_ref[s])
                    def _():
                        unsort((g,))

    @pl.when(half == COMB_SPLIT - 1)
    def _():
        g2 = mod_ref[:, 5 * D_MODEL:6 * D_MODEL]
        y = _layer_norm(alpha * x1_ref[...] + g2 * acc_scr[...])
        o_ref[...] = y * lng_ref[1:2, :] + lnb_ref[1:2, :]


def _combine(used_rows, ys, sel_t, h2, x1, mod_l, w_sh_in_l, w_sh_out_l, ln_g_l, ln_b_l, alpha, cond_of_tile,
             tile0=0, n_tiles=None):
    nt = h2.shape[0] // MOE_TILE - tile0 if n_tiles is None else n_tiles
    nrc = COMB_SPLIT
    comb_rows = TILE_ROWS // COMB_SPLIT
    nc = mod_l.shape[0]
    kern = functools.partial(_combine_kernel, alpha=alpha, tile0=tile0)
    row = lambda s, r, u: (s + tile0, 0)
    const = lambda s, r, u: (0, 0)
    return pl.pallas_call(
        kern,
        grid_spec=pltpu.PrefetchScalarGridSpec(
            num_scalar_prefetch=1,
            grid=(nt, nrc),
            in_specs=[
                pl.BlockSpec((comb_rows, D_MODEL), lambda s, r, u: ((s + tile0) * nrc + r, 0)),
                pl.BlockSpec((N_EXPERTS, MOE_TILE), lambda s, r, u: (0, s + tile0)),
                pl.BlockSpec((MOE_TILE, D_MODEL), row),
                pl.BlockSpec((MOE_TILE, D_MODEL), row),
                pl.BlockSpec((None, 1, 6 * D_MODEL), lambda s, r, u: (cond_of_tile(s + tile0), 0, 0)),
                pl.BlockSpec((D_MODEL, 2 * D_SHARED), const),
                pl.BlockSpec((D_SHARED, D_MODEL), const),
                pl.BlockSpec((2, D_MODEL), const),
                pl.BlockSpec((2, D_MODEL), const),
            ],
            out_specs=pl.BlockSpec((MOE_TILE, D_MODEL), lambda s, r, u: (s, 0)),
            scratch_shapes=[
                pltpu.VMEM((MOE_TILE, D_MODEL), F32),
                pltpu.VMEM((2 * MOE_TILE, N_EXPERTS), BF16),
                pltpu.VMEM((8, N_EXPERTS), F32),
            ],
        ),
        out_shape=jax.ShapeDtypeStruct((nt * MOE_TILE, D_MODEL), F32),
        compiler_params=_cparams(("arbitrary", "arbitrary")),
        name="moe_combine",
    )(used_rows, ys, sel_t, h2, x1, mod_l.reshape(nc, 1, 6 * D_MODEL), w_sh_in_l, w_sh_out_l, ln_g_l, ln_b_l)


def _block_lists(nb):
    nt = nb.shape[0]
    cum = jnp.cumsum(nb, axis=1)
    boff = cum - nb
    used_blk = cum[:, -1]
    per_e = jnp.sum(nb, axis=0)
    bstart = jnp.concatenate([jnp.zeros((1,), jnp.int32), jnp.cumsum(per_e).astype(jnp.int32)])
    pref = jnp.cumsum(nb, axis=0) - nb
    lb = jnp.arange(TILE_BLKS, dtype=jnp.int32)
    owner = (lb[None, :, None] >= boff[:, None, :]) & (lb[None, :, None] < cum[:, None, :])
    base = bstart[None, :-1] + pref - boff
    pos = jnp.sum(jnp.where(owner, base[:, None, :], 0), axis=-1) + lb[None, :]
    total = nt * TILE_BLKS
    pos = jnp.where(jnp.any(owner, axis=-1), pos, total)
    src = (jnp.arange(nt, dtype=jnp.int32)[:, None] * TILE_BLKS + lb[None, :]).reshape(-1)
    bsrc = jnp.zeros((total,), jnp.int32).at[pos.reshape(-1)].set(src, mode="drop")
    return (used_blk * ROW_BLK).astype(jnp.int32), bstart, bsrc


def _lower_bound_consts(lb_raw):
    p = jax.nn.softmax(lb_raw.astype(F32), axis=0)
    cs = jnp.cumsum(p, axis=0)
    lb = jnp.clip(jnp.concatenate([jnp.zeros_like(cs[:1]), cs[:-1]], axis=0), 0.0, LB_MAX)
    rows = jnp.stack([jnp.log(lb[:, 0]), jnp.log1p(-lb[:, 0]), 1.0 - lb[:, 0],
                      jnp.log(lb[:, 1]), jnp.log1p(-lb[:, 1]), 1.0 - lb[:, 1]], axis=1)
    return jnp.concatenate([rows, jnp.zeros((rows.shape[0], 2, W_R), F32)], axis=1)


def _rope_tables(seq_len):
    t = jnp.arange(seq_len)
    half = HEAD_DIM // 2
    nf = half // 2
    inv = ROPE_THETA ** (-jnp.arange(nf, dtype=F32) / nf)

    def tabs(pos):
        ang = pos[:, None].astype(F32) * inv[None, :]
        cos = jnp.concatenate([jnp.cos(ang), jnp.cos(ang)], -1)
        sin = jnp.sin(ang)
        zero = jnp.zeros_like(sin)
        return cos, jnp.concatenate([-sin, zero], -1), jnp.concatenate([zero, sin], -1)

    row = tabs(t // GRID_W)
    colt = tabs(t % GRID_W)
    return tuple(jnp.concatenate([a, b], -1) for a, b in zip(row, colt))


def _bias_tables(rpb):
    cq = jnp.arange(GRID_W)
    cs = jnp.clip(cq - WIN_C // 2, 0, GRID_W - WIN_C)
    valid = (cq[None, :] >= cs[:, None]) & (cq[None, :] < cs[:, None] + WIN_C)
    coff = jnp.clip(cq[None, :] - cq[:, None], -(WIN_C - 1), WIN_C - 1) + WIN_C - 1
    roff = jnp.arange(WIN_R)[:, None] + jnp.arange(WIN_R)[None, :]
    pick_c = (coff[None] == jnp.arange(2 * WIN_C - 1)[:, None, None]).astype(F32)
    pick_r = (roff[None] == jnp.arange(2 * WIN_R - 1)[:, None, None]).astype(F32)
    hi = lax.Precision.HIGHEST
    bias_c = jnp.einsum('dhrc,cqw->dhrqw', rpb.astype(F32), pick_c, precision=hi)
    tab = jnp.einsum('rlk,dhrqw->dhlqkw', pick_r, bias_c, precision=hi)
    tab = jnp.where(valid[None, None, None, :, None, :], tab, NEG_BIG)
    return tab.reshape(tab.shape[0], H_A, WIN_R, GRID_W, WIN_R * GRID_W)


def kernel(x_prompt, x_sample, cache_k, cache_v, state_hgrn, c, c_ctx, w_ada, b_ada, w_in, w_out, lb_raw, hgrn_norm,
           rpb, ln_g, ln_b, w_router, router_bias, w_e_in, w_e_out, w_sh_in, w_sh_out):
    n_p, seq, d = x_prompt.shape
    n_s, dseq, _ = x_sample.shape
    depth = w_in.shape[0]
    t_p, t_s = n_p * seq, n_s * dseq
    t = t_p + t_s
    alpha = (2 * depth) ** 0.25
    assert d == D_MODEL and seq % MOE_TILE == 0 and dseq % (2 * MOE_TILE) == 0 and t_p % dseq == 0
    assert dseq % GRID_W == 0 and dseq // GRID_W >= WIN_R

    tm_in = next(m for m in (1024, 512, 256) if t_p % m == 0 and dseq % m == 0)

    def cond_of_tile(tile_rows):
        n_p_tiles = t_p // tile_rows
        per_seq = dseq // tile_rows
        return lambda i: jnp.where(i < n_p_tiles, 0, 1 + (i - n_p_tiles) // per_seq)

    n_cond = -(-(1 + n_s) // 8) * 8
    cond = jnp.zeros((n_cond, d), F32).at[0].set(c_ctx).at[1:1 + n_s].set(c)
    mod = _modulation(cond, w_ada, b_ada)

    lbc = _lower_bound_consts(lb_raw)
    rope_tabs = _rope_tables(dseq)
    bias_tabs = _bias_tables(rpb)
    w_in_b = w_in.astype(BF16)
    w_out_b = w_out.astype(BF16)
    w_sh_in_b = w_sh_in.astype(BF16)
    w_sh_out_b = w_sh_out.astype(BF16)
    wr_t = jnp.swapaxes(w_router, 1, 2).astype(BF16)

    x = jnp.concatenate([x_prompt.reshape(t_p, d), x_sample.reshape(t_s, d)], axis=0)
    new_k, new_v, new_s = [], [], []
    for l in range(depth):
        proj = _inproj(x, mod[l], w_in_b[l], tm_in, cond_of_tile(tm_in))
        nrm = hgrn_norm[l].reshape(1, HEAD_DIM)
        or_p, st_p = _hgrn(proj, lbc[l], nrm, seq, n_p, 0, emit_state=True)
        (or_s,) = _hgrn(proj, lbc[l], nrm, dseq, n_s, t_p // dseq, state_in=state_hgrn, layer=l)
        oa_p, k_l, v_l = _ctx_attention(proj, seq, n_p)
        oa_s = _nbr_attention(proj, cache_k, cache_v, bias_tabs[l], rope_tabs, l, dseq, n_s, t_p // dseq)
        x1, h2, w_t, sel_t, nb = _outproj(or_p, or_s, oa_p, oa_s, x, mod[l], w_out_b[l], ln_g[l], ln_b[l], wr_t[l],
                                          router_bias[l].reshape(N_EXPERTS, 1), alpha, cond_of_tile(2 * MOE_TILE))
        used_rows, bstart, bsrc = _block_lists(nb[:, 0, :].astype(jnp.int32))
        xs = _dispatch(used_rows, h2, sel_t, w_t)
        ys = _experts(bstart, bsrc, xs, w_e_in, w_e_out, l)
        comb = functools.partial(_combine, used_rows, ys, sel_t, h2, x1, mod[l], w_sh_in_b[l], w_sh_out_b[l],
                                 ln_g[l], ln_b[l], alpha, cond_of_tile(MOE_TILE))
        if l + 1 < depth:
            x = comb()
        else:
            y_p = comb(tile0=0, n_tiles=t_p // MOE_TILE)
            y_s = comb(tile0=t_p // MOE_TILE, n_tiles=t_s // MOE_TILE)
        new_k.append(k_l)
        new_v.append(v_l)
        new_s.append(st_p)
    return (y_p.reshape(n_p, seq, d), y_s.reshape(n_s, dseq, d),
            jnp.stack(new_k, axis=1), jnp.stack(new_v, axis=1), jnp.stack(new_s, axis=1))
```

```python
import functools

import jax
import jax.numpy as jnp
from jax import lax
from jax.experimental import pallas as pl
from jax.experimental.pallas import tpu as pltpu

F32 = jnp.float32
BF16 = jnp.bfloat16

D_MODEL = 2048
HEAD_DIM = 128
N_HEADS = D_MODEL // HEAD_DIM
H_R = N_HEADS // 2
H_A = N_HEADS - H_R
W_R = H_R * HEAD_DIM
W_A = H_A * HEAD_DIM
N_COLS = 5 * W_R + 3 * W_A
N_COLBLK = N_COLS // HEAD_DIM
CHUNK = 64
SUB = 16
GRID_W = 64
WIN_R = 8
WIN_C = 16
ROPE_THETA = 10000.0
N_EXPERTS = 64
TOP_K = 8
N_GROUPS = 8
GROUP_SIZE = N_EXPERTS // N_GROUPS
TOPK_GROUPS = 4
D_EXPERT = 512
D_SHARED = 512
ROUTED_SCALE = 2.5
EPS = 1e-6
LB_MAX = 1.0 - 1e-4
NEG_BIG = -1e30
EXP_CLAMP = 60.0

MOE_TILE = 256
ROW_BLK = 16
TILE_ROWS = MOE_TILE * TOP_K + N_EXPERTS * ROW_BLK
TILE_BLKS = TILE_ROWS // ROW_BLK
W_LANES = 256
XS_COLS = D_MODEL + W_LANES
DISP_ROWS = 512
COMB_SPLIT = 2
EXP_GROUP = 32
VMEM_LIMIT = 52 * 1024 * 1024


def _cparams(sem):
    return pltpu.CompilerParams(dimension_semantics=sem, vmem_limit_bytes=VMEM_LIMIT)


def _dot(a, b):
    return jnp.dot(a, b, preferred_element_type=F32)


def _dot_nt(a, b):
    return lax.dot_general(a, b, (((1,), (1,)), ((), ())), preferred_element_type=F32)


def _dot_tn(a, b):
    return lax.dot_general(a, b, (((0,), (0,)), ((), ())), preferred_element_type=F32)


def _silu(x):
    return x * jax.nn.sigmoid(x)


def _layer_norm(x):
    mu = jnp.mean(x, axis=-1, keepdims=True)
    xc = x - mu
    var = jnp.mean(xc * xc, axis=-1, keepdims=True)
    return xc * lax.rsqrt(var + EPS)


def _split3(x):
    a = x.astype(BF16)
    r = x - a.astype(F32)
    b = r.astype(BF16)
    c = (r - b.astype(F32)).astype(BF16)
    return a, b, c


def _mod_kernel(c_ref, w_ref, b_ref, o_ref):
    a = _silu(c_ref[...]).astype(BF16)
    o_ref[...] = _dot(a, w_ref[...].astype(BF16)) + b_ref[...]


def _modulation(cond, w_ada, b_ada):
    depth, d, n6 = w_ada.shape
    nc = cond.shape[0]
    tn = 1024
    return pl.pallas_call(
        _mod_kernel,
        grid=(depth, n6 // tn),
        in_specs=[
            pl.BlockSpec((nc, d), lambda l, j: (0, 0)),
            pl.BlockSpec((None, d, tn), lambda l, j: (l, 0, j)),
            pl.BlockSpec((None, 1, tn), lambda l, j: (l, 0, j)),
        ],
        out_specs=pl.BlockSpec((None, nc, tn), lambda l, j: (l, 0, j)),
        out_shape=jax.ShapeDtypeStruct((depth, nc, n6), F32),
        compiler_params=_cparams(("arbitrary", "arbitrary")),
        name="modulation",
    )(cond, w_ada, b_ada.reshape(depth, 1, n6))


def _inproj_kernel(x_ref, mod_ref, w_ref, o_ref, h_scr):
    @pl.when(pl.program_id(1) == 0)
    def _():
        y = _layer_norm(x_ref[...])
        sh = mod_ref[:, 0:D_MODEL]
        sc = mod_ref[:, D_MODEL:2 * D_MODEL]
        h_scr[...] = (y * (1.0 + sc) + sh).astype(BF16)

    res = _dot(h_scr[...], w_ref[...]).astype(BF16)
    for j in range(o_ref.shape[0]):
        o_ref[j] = res[:, j * HEAD_DIM:(j + 1) * HEAD_DIM]


def _inproj(x, mod_l, w_in_l, tm, cond_of_tile):
    t = x.shape[0]
    tn = 1024
    nblk = tn // HEAD_DIM
    nc = mod_l.shape[0]
    return pl.pallas_call(
        _inproj_kernel,
        grid=(t // tm, N_COLS // tn),
        in_specs=[
            pl.BlockSpec((tm, D_MODEL), lambda i, j: (i, 0)),
            pl.BlockSpec((None, 1, 6 * D_MODEL), lambda i, j: (cond_of_tile(i), 0, 0)),
            pl.BlockSpec((D_MODEL, tn), lambda i, j: (0, j)),
        ],
        out_specs=pl.BlockSpec((nblk, tm, HEAD_DIM), lambda i, j: (j, i, 0)),
        out_shape=jax.ShapeDtypeStruct((N_COLBLK, t, HEAD_DIM), BF16),
        scratch_shapes=[pltpu.VMEM((tm, D_MODEL), BF16)],
        compiler_params=_cparams(("arbitrary", "arbitrary")),
        name="inproj",
    )(x, mod_l.reshape(nc, 1, 6 * D_MODEL), w_in_l)


def _gates(z, log_lb, log1m_lb, one_m_lb):
    lse = jnp.log(1.0 + jnp.exp(-jnp.abs(z)))
    log_sig = jnp.minimum(z, 0.0) - lse
    k = one_m_lb * jnp.exp(log_sig - z)
    c = log1m_lb + log_sig
    g = jnp.maximum(log_lb, c) + jnp.log(1.0 + jnp.exp(-jnp.abs(log_lb - c)))
    return k, g


def _chunk_cumsum(tri, g):
    r = _dot(tri, jnp.concatenate(_split3(g), axis=1))
    return r[:, 0:HEAD_DIM] + r[:, HEAD_DIM:2 * HEAD_DIM] + r[:, 2 * HEAD_DIM:3 * HEAD_DIM]


def _intra_scores(q, k, b, reverse):
    zeros = jnp.zeros((SUB, HEAD_DIM), F32)
    nsub = CHUNK // SUB
    lhs_rows, rhs_slots = [], []
    for blk in range(nsub):
        lo = blk * SUB
        mid = lo + SUB // 2 if reverse else lo + SUB // 2 - 1
        r = b[mid:mid + 1, :]
        qe = q[lo:lo + SUB] * jnp.exp(jnp.minimum(b[lo:lo + SUB] - r, EXP_CLAMP))
        lhs_rows.append(jnp.concatenate([qe if j == blk else zeros for j in range(nsub)], axis=1))
        ks, ke = (lo, CHUNK) if reverse else (0, lo + SUB)
        kx = k[ks:ke] * jnp.exp(jnp.minimum(r - b[ks:ke], EXP_CLAMP))
        pad_lo = [jnp.zeros((ks, HEAD_DIM), F32)] if ks else []
        pad_hi = [jnp.zeros((CHUNK - ke, HEAD_DIM), F32)] if ke < CHUNK else []
        rhs_slots.append(jnp.concatenate(pad_lo + [kx] + pad_hi, axis=0) if pad_lo or pad_hi else kx)
    lhs = jnp.concatenate(lhs_rows, axis=0).astype(BF16)
    rhs = jnp.concatenate(rhs_slots, axis=1).astype(BF16)
    return _dot_nt(lhs, rhs)


def _intra_scores_exact(q, k, b, reverse):
    s_idx = lax.broadcasted_iota(jnp.int32, (CHUNK, HEAD_DIM), 0)
    r_idx = lax.broadcasted_iota(jnp.int32, (SUB, HEAD_DIM), 0)
    zeros = jnp.zeros((SUB, HEAD_DIM), F32)
    out = []
    for blk in range(CHUNK // SUB):
        lo = blk * SUB
        qb = q[lo:lo + SUB]
        lhs_slots, rhs_slots = [], []
        for j in range(SUB):
            t = lo + j
            valid = (s_idx >= t) if reverse else (s_idx <= t)
            rhs_slots.append(k * jnp.exp(jnp.where(valid, b[t:t + 1, :] - b, NEG_BIG)))
            lhs_slots.append(jnp.where(r_idx == j, qb, zeros))
        lhs = jnp.concatenate(lhs_slots, axis=1).astype(BF16)
        rhs = jnp.concatenate(rhs_slots, axis=1).astype(BF16)
        out.append(_dot_nt(lhs, rhs))
    return jnp.concatenate(out, axis=0)


def _hgrn_kernel(*refs, seq_len, has_init, emit_state):
    q_ref, i_ref, zf_ref, zb_ref, g_ref, lbc_ref, nrm_ref = refs[:7]
    pos = 7
    s0_ref = None
    if has_init:
        s0_ref = refs[pos]
        pos += 1
    o_ref = refs[pos]
    pos += 1
    so_ref = None
    if emit_state:
        so_ref = refs[pos]
        pos += 1
    o_scr, qd_scr, ut_scr, dec_scr = refs[pos:pos + 4]

    n_chunks = seq_len // CHUNK
    ri = lax.broadcasted_iota(jnp.int32, (CHUNK, CHUNK), 0)
    ci = lax.broadcasted_iota(jnp.int32, (CHUNK, CHUNK), 1)
    tri_f = jnp.where(ci <= ri, 1.0, 0.0).astype(BF16)
    tri_b = jnp.where(ci >= ri, 1.0, 0.0).astype(BF16)
    lbc = lbc_ref[...]
    lbc_f = (lbc[0:1], lbc[1:2], lbc[2:3])
    lbc_b = (lbc[3:4], lbc[4:5], lbc[5:6])

    if has_init:
        st_f0 = s0_ref[0].T
        st_b0 = s0_ref[1].T
    else:
        st_f0 = jnp.zeros((HEAD_DIM, HEAD_DIM), F32)
        st_b0 = jnp.zeros((HEAD_DIM, HEAD_DIM), F32)

    per_it = 4
    dirs = ((zf_ref, lbc_f, tri_f, ci <= ri, False), (zb_ref, lbc_b, tri_b, ci >= ri, True))

    def decay_floor(z_ref, lbc_d):
        zmin = jnp.min(z_ref[...], axis=0, keepdims=True).astype(F32)
        return jnp.min(jnp.maximum(lbc_d[0], lbc_d[1] + jnp.minimum(zmin, 0.0) - 0.6931472))

    safe = jnp.minimum(decay_floor(zf_ref, lbc_f), decay_floor(zb_ref, lbc_b)) >= -EXP_CLAMP / (SUB // 2)

    def local(it, carry, scores):
        chains = []
        for u in range(per_it):
            c = it * per_it + u
            rows = pl.ds(pl.multiple_of(c * CHUNK, CHUNK), CHUNK)
            q = _silu(q_ref[rows, :].astype(F32))
            v = i_ref[rows, :]
            for d, (z_ref, lbc_d, tri, keep, rev) in enumerate(dirs):
                k, g = _gates(z_ref[rows, :].astype(F32), lbc_d[0], lbc_d[1], lbc_d[2])
                chains.append(dict(c=c, rows=rows, d=d, q=q, v=v, k=k, g=g, tri=tri, keep=keep, rev=rev))
        for ch in chains:
            ch["b"] = _chunk_cumsum(ch["tri"], ch["g"])
        for ch in chains:
            ch["a"] = scores(ch["q"], ch["k"], ch["b"], ch["rev"])
        for ch in chains:
            d, rows, b = ch["d"], ch["rows"], ch["b"]
            a = jnp.where(ch["keep"], ch["a"], 0.0).astype(BF16)
            o_scr[d, rows, :] = _dot(a, ch["v"])
            qd_scr[d, rows, :] = (ch["q"] * jnp.exp(b)).astype(BF16)
            edge = b[0:1, :] if ch["rev"] else b[CHUNK - 1:CHUNK, :]
            ut_scr[d, ch["c"]] = _dot_tn(ch["v"], (ch["k"] * jnp.exp(edge - b)).astype(BF16))
            dec_scr[d, ch["c"]] = jnp.exp(edge)
        return carry

    @pl.when(safe)
    def _():
        lax.fori_loop(0, n_chunks // per_it, functools.partial(local, scores=_intra_scores), 0)

    @pl.when(jnp.logical_not(safe))
    def _():
        lax.fori_loop(0, n_chunks // per_it, functools.partial(local, scores=_intra_scores_exact), 0)

    def scan(j, carry):
        new = []
        for d, st in enumerate(carry):
            c = j if d == 0 else n_chunks - 1 - j
            upd = ut_scr[d, c]
            ut_scr[d, c] = st
            new.append(st * dec_scr[d, c] + upd)
        return tuple(new)

    st_f, st_b = lax.fori_loop(0, n_chunks, scan, (st_f0, st_b0), unroll=2)

    per_it3 = 4

    def inter(it, carry):
        prods = []
        for u in range(per_it3):
            c = it * per_it3 + u
            rows = pl.ds(pl.multiple_of(c * CHUNK, CHUNK), CHUNK)
            for d in range(2):
                prods.append((d, rows, _dot_nt(qd_scr[d, rows, :], ut_scr[d, c].astype(BF16))))
        for d, rows, p in prods:
            o_scr[d, rows, :] += p
        return carry

    lax.fori_loop(0, n_chunks // per_it3, inter, 0)
    if emit_state:
        so_ref[0] = st_f.T
        so_ref[1] = st_b.T

    nrm = nrm_ref[...]
    piece = 256 if seq_len % 256 == 0 else CHUNK

    def fin(p, carry):
        r0 = pl.multiple_of(p * piece, piece)
        o = o_scr[0, pl.ds(r0, piece), :] + o_scr[1, pl.ds(r0, piece), :]
        o = o * lax.rsqrt(jnp.mean(o * o, axis=-1, keepdims=True) + EPS) * nrm
        gate = g_ref[pl.ds(r0, piece), :].astype(F32)
        o_ref[pl.ds(r0, piece), :] = (o * _silu(gate)).astype(BF16)
        return carry

    lax.fori_loop(0, seq_len // piece, fin, 0)


def _hgrn(proj, lbc, nrm, seq_len, n_seq, row_blk0, state_in=None, layer=0, emit_state=False):
    kern = functools.partial(_hgrn_kernel, seq_len=seq_len, has_init=state_in is not None, emit_state=emit_state)

    def col(sec):
        return pl.BlockSpec((None, seq_len, HEAD_DIM), lambda b, h: (sec * H_R + h, row_blk0 + b, 0))

    in_specs = [col(0), col(1), col(2), col(3), col(4),
                pl.BlockSpec((8, HEAD_DIM), lambda b, h: (0, h)),
                pl.BlockSpec((1, HEAD_DIM), lambda b, h: (0, 0))]
    args = [proj, proj, proj, proj, proj, lbc, nrm]
    if state_in is not None:
        in_specs.append(pl.BlockSpec((None, None, 2, None, HEAD_DIM, HEAD_DIM),
                                     lambda b, h: (b, layer, 0, h, 0, 0)))
        args.append(state_in)
    out_specs = [pl.BlockSpec((seq_len, HEAD_DIM), lambda b, h: (b, h))]
    out_shape = [jax.ShapeDtypeStruct((n_seq * seq_len, W_R), BF16)]
    if emit_state:
        out_specs.append(pl.BlockSpec((None, 2, None, HEAD_DIM, HEAD_DIM), lambda b, h: (b, 0, h, 0, 0)))
        out_shape.append(jax.ShapeDtypeStruct((n_seq, 2, H_R, HEAD_DIM, HEAD_DIM), F32))
    res = pl.pallas_call(
        kern,
        grid=(n_seq, H_R),
        in_specs=in_specs,
        out_specs=out_specs,
        out_shape=out_shape,
        scratch_shapes=[pltpu.VMEM((2, seq_len, HEAD_DIM), F32),
                        pltpu.VMEM((2, seq_len, HEAD_DIM), BF16),
                        pltpu.VMEM((2, seq_len // CHUNK, HEAD_DIM, HEAD_DIM), F32),
                        pltpu.VMEM((2, seq_len // CHUNK, 1, HEAD_DIM), F32)],
        compiler_params=_cparams(("arbitrary", "arbitrary")),
        name="hgrn_state" if emit_state else "hgrn",
    )(*args)
    return res


def _ctx_attn_kernel(q_ref, k_ref, v_ref, o_ref, ko_ref, vo_ref):
    heads = range(H_A)
    scores = [_dot_nt(q_ref[h], k_ref[h]) * (HEAD_DIM ** -0.5) for h in heads]
    probs, dens = [], []
    for s in scores:
        p = jnp.exp(s - jnp.max(s, axis=-1, keepdims=True))
        dens.append(jnp.sum(p, axis=-1, keepdims=True))
        probs.append(p.astype(BF16))
    for h in heads:
        o = _dot(probs[h], v_ref[h]) / dens[h]
        o_ref[:, h * HEAD_DIM:(h + 1) * HEAD_DIM] = o.astype(BF16)
        ko_ref[h] = k_ref[h].astype(F32)
        vo_ref[h] = v_ref[h].astype(F32)


def _ctx_attention(proj, seq_len, n_seq):
    def col(sec):
        return pl.BlockSpec((H_A, seq_len, HEAD_DIM), lambda b: ((5 * H_R + sec * H_A) // H_A, b, 0))

    cache_spec = pl.BlockSpec((None, H_A, seq_len, HEAD_DIM), lambda b: (b, 0, 0, 0))
    cache_shape = jax.ShapeDtypeStruct((n_seq, H_A, seq_len, HEAD_DIM), F32)
    return pl.pallas_call(
        _ctx_attn_kernel,
        grid=(n_seq,),
        in_specs=[col(0), col(1), col(2)],
        out_specs=[pl.BlockSpec((seq_len, W_A), lambda b: (b, 0)), cache_spec, cache_spec],
        out_shape=[jax.ShapeDtypeStruct((n_seq * seq_len, W_A), BF16), cache_shape, cache_shape],
        compiler_params=_cparams(("arbitrary",)),
        name="ctx_attention",
    )(proj, proj, proj)


def _rope(x, cos, sin_lo, sin_hi):
    return (x * cos + pltpu.roll(x, HEAD_DIM - HEAD_DIM // 4, axis=1) * sin_lo
            + pltpu.roll(x, HEAD_DIM // 4, axis=1) * sin_hi)


def _nbr_attn_kernel(q_ref, k_ref, v_ref, kc_ref, vc_ref, bias_ref, cos_ref, slo_ref, shi_ref, o_ref, kr_scr,
                     *, n_rows):
    rows_pc = 256

    def rope_k(p, carry):
        r0 = pl.multiple_of(p * rows_pc, rows_pc)
        sl = pl.ds(r0, rows_pc)
        kr_scr[sl, :] = _rope(k_ref[sl, :].astype(F32), cos_ref[sl, :], slo_ref[sl, :], shi_ref[sl, :]).astype(BF16)
        return carry

    lax.fori_loop(0, (n_rows * GRID_W) // rows_pc, rope_k, 0)
    kc = kc_ref[...].astype(BF16)
    vc = vc_ref[...].astype(BF16)
    scale = HEAD_DIM ** -0.5
    win = WIN_R * GRID_W

    per_it = 8 if n_rows % 8 == 0 else 1

    def rows_step(it, carry):
        items = []
        for u in range(per_it):
            r = it * per_it + u
            rs = jnp.clip(r - WIN_R // 2, 0, n_rows - WIN_R)
            qs = pl.ds(pl.multiple_of(r * GRID_W, GRID_W), GRID_W)
            ks = pl.ds(pl.multiple_of(rs * GRID_W, GRID_W), win)
            q = _rope(q_ref[qs, :].astype(F32), cos_ref[qs, :], slo_ref[qs, :], shi_ref[qs, :])
            items.append(dict(qs=qs, ks=ks, dl=rs - r + WIN_R - 1, q=(q * scale).astype(BF16)))
        for x in items:
            x["s_loc"] = _dot_nt(x["q"], kr_scr[x["ks"], :]) + bias_ref[x["dl"]]
            x["s_ctx"] = _dot_nt(x["q"], kc)
        for x in items:
            m = jnp.maximum(jnp.max(x["s_loc"], axis=-1, keepdims=True), jnp.max(x["s_ctx"], axis=-1, keepdims=True))
            p_loc = jnp.exp(x["s_loc"] - m)
            p_ctx = jnp.exp(x["s_ctx"] - m)
            x["den"] = jnp.sum(p_loc, axis=-1, keepdims=True) + jnp.sum(p_ctx, axis=-1, keepdims=True)
            x["p_loc"] = p_loc.astype(BF16)
            x["p_ctx"] = p_ctx.astype(BF16)
        for x in items:
            o = _dot(x["p_loc"], v_ref[x["ks"], :]) + _dot(x["p_ctx"], vc)
            o_ref[x["qs"], :] = (o / x["den"]).astype(BF16)
        return carry

    lax.fori_loop(0, n_rows // per_it, rows_step, 0)


def _nbr_attention(proj, cache_k, cache_v, bias_tab, rope_tabs, layer, seq_len, n_seq, row_blk0):
    n_rows = seq_len // GRID_W
    past = cache_k.shape[3]
    kern = functools.partial(_nbr_attn_kernel, n_rows=n_rows)

    def col(sec):
        return pl.BlockSpec((None, seq_len, HEAD_DIM), lambda b, h: (5 * H_R + sec * H_A + h, row_blk0 + b, 0))

    cache_spec = pl.BlockSpec((None, None, None, past, HEAD_DIM), lambda b, h: (b, layer, h, 0, 0))
    tab_spec = pl.BlockSpec((seq_len, HEAD_DIM), lambda b, h: (0, 0))
    return pl.pallas_call(
        kern,
        grid=(n_seq, H_A),
        in_specs=[col(0), col(1), col(2), cache_spec, cache_spec,
                  pl.BlockSpec((None, WIN_R, GRID_W, WIN_R * GRID_W), lambda b, h: (h, 0, 0, 0)),
                  tab_spec, tab_spec, tab_spec],
        out_specs=pl.BlockSpec((seq_len, HEAD_DIM), lambda b, h: (b, h)),
        out_shape=jax.ShapeDtypeStruct((n_seq * seq_len, W_A), BF16),
        scratch_shapes=[pltpu.VMEM((seq_len, HEAD_DIM), BF16)],
        compiler_params=_cparams(("arbitrary", "arbitrary")),
        name="nbr_attention",
    )(proj, proj, proj, cache_k, cache_v, bias_tab, *rope_tabs)


def _first_argmax_rows(vals, ids, sentinel):
    m = vals[0]
    for v in vals[1:]:
        m = jnp.maximum(m, v)
    m = jnp.max(m, axis=0, keepdims=True)
    best = None
    for v, i in zip(vals, ids):
        cand = jnp.min(jnp.where(v == m, i, sentinel), axis=0, keepdims=True)
        best = cand if best is None else jnp.minimum(best, cand)
    return m, best


def _route(s, sb):
    n = s.shape[1]
    iota8 = lax.broadcasted_iota(jnp.int32, (GROUP_SIZE, n), 0)
    neg_inf = -jnp.inf
    groups = [sb[g * GROUP_SIZE:(g + 1) * GROUP_SIZE, :] for g in range(N_GROUPS)]
    gscore = jnp.zeros((N_GROUPS, n), F32)
    for g, xg in enumerate(groups):
        m1, i1 = _first_argmax_rows([xg], [iota8], GROUP_SIZE)
        m2 = jnp.max(jnp.where(iota8 == i1, neg_inf, xg), axis=0, keepdims=True)
        gscore = jnp.where(iota8 == g, m1 + m2, gscore)
    gsel = jnp.zeros((N_GROUPS, n), F32)
    cur = gscore
    for _ in range(TOPK_GROUPS):
        _, i = _first_argmax_rows([cur], [iota8], N_GROUPS)
        hit = iota8 == i
        gsel = jnp.where(hit, 1.0, gsel)
        cur = jnp.where(hit, neg_inf, cur)
    cur = [jnp.where(gsel[g:g + 1, :] > 0.5, xg, neg_inf) for g, xg in enumerate(groups)]
    ids = [iota8 + g * GROUP_SIZE for g in range(N_GROUPS)]
    sel = [jnp.zeros((GROUP_SIZE, n), jnp.bool_) for _ in range(N_GROUPS)]
    for _ in range(TOP_K):
        _, i = _first_argmax_rows(cur, ids, N_EXPERTS)
        hits = [idg == i for idg in ids]
        sel = [jnp.logical_or(a, h) for a, h in zip(sel, hits)]
        cur = [jnp.where(h, neg_inf, c) for c, h in zip(cur, hits)]
    return sel


def _outproj_kernel(orp_ref, ors_ref, oap_ref, oas_ref, x_ref, mod_ref, w_ref, lng_ref, lnb_ref, wr_ref, rb_ref,
                    x1_ref, h2_ref, wt_ref, sel_ref, nb_ref, *, alpha, n_prompt_tiles):
    from_prompt = pl.program_id(0) < n_prompt_tiles
    n = MOE_TILE
    tiles = [slice(i * n, (i + 1) * n) for i in range(x_ref.shape[0] // n)]
    mixes = []
    for sl in tiles:
        o_r = jnp.where(from_prompt, orp_ref[sl, :], ors_ref[sl, :])
        o_a = jnp.where(from_prompt, oap_ref[sl, :], oas_ref[sl, :])
        mixes.append(_dot(o_r, w_ref[0:W_R, :]) + _dot(o_a, w_ref[W_R:W_R + W_A, :]))
    g1 = mod_ref[:, 2 * D_MODEL:3 * D_MODEL]
    sh2 = mod_ref[:, 3 * D_MODEL:4 * D_MODEL]
    sc2 = mod_ref[:, 4 * D_MODEL:5 * D_MODEL]
    scores = []
    for sl, mix in zip(tiles, mixes):
        x1 = _layer_norm(alpha * x_ref[sl, :] + g1 * mix) * lng_ref[0:1, :] + lnb_ref[0:1, :]
        x1_ref[sl, :] = x1
        h2 = (_layer_norm(x1) * (1.0 + sc2) + sh2).astype(BF16)
        h2_ref[sl, :] = h2
        scores.append(jax.nn.sigmoid(_dot_nt(wr_ref[...], h2)))
    for i, (sl, s) in enumerate(zip(tiles, scores)):
        sel = _route(s, s + rb_ref[...])
        wsel = [jnp.where(sel[g], s[g * GROUP_SIZE:(g + 1) * GROUP_SIZE, :], 0.0) for g in range(N_GROUPS)]
        tot = wsel[0]
        for w in wsel[1:]:
            tot = tot + w
        tot = jnp.sum(tot, axis=0, keepdims=True)
        for g in range(N_GROUPS):
            rows = slice(g * GROUP_SIZE, (g + 1) * GROUP_SIZE)
            wt_ref[rows, sl] = wsel[g] / tot * ROUTED_SCALE
            sel_ref[rows, sl] = jnp.where(sel[g], 1.0, 0.0)
        cnt = _dot_nt(jnp.ones((8, n), BF16), sel_ref[:, sl].astype(BF16))
        nb_ref[i] = jnp.floor((cnt + (ROW_BLK - 1)) * (1.0 / ROW_BLK))


def _outproj(or_p, or_s, oa_p, oa_s, x, mod_l, w_out_l, ln_g_l, ln_b_l, wr_t, rbias, alpha, cond_of_tile):
    t = x.shape[0]
    per_step = 2
    tm = per_step * MOE_TILE
    nt = t // tm
    n_p = or_p.shape[0] // tm
    nc = mod_l.shape[0]
    kern = functools.partial(_outproj_kernel, alpha=alpha, n_prompt_tiles=n_p)
    row = lambda i: (i, 0)
    const = lambda i: (0, 0)
    prompt_row = lambda i: (jnp.minimum(i, n_p - 1), 0)
    sample_row = lambda i: (jnp.maximum(i - n_p, 0), 0)
    return pl.pallas_call(
        kern,
        grid=(nt,),
        in_specs=[
            pl.BlockSpec((tm, W_R), prompt_row),
            pl.BlockSpec((tm, W_R), sample_row),
            pl.BlockSpec((tm, W_A), prompt_row),
            pl.BlockSpec((tm, W_A), sample_row),
            pl.BlockSpec((tm, D_MODEL), row),
            pl.BlockSpec((None, 1, 6 * D_MODEL), lambda i: (cond_of_tile(i), 0, 0)),
            pl.BlockSpec((W_R + W_A, D_MODEL), const),
            pl.BlockSpec((2, D_MODEL), const),
            pl.BlockSpec((2, D_MODEL), const),
            pl.BlockSpec((N_EXPERTS, D_MODEL), const),
            pl.BlockSpec((N_EXPERTS, 1), const),
        ],
        out_specs=[
            pl.BlockSpec((tm, D_MODEL), row),
            pl.BlockSpec((tm, D_MODEL), row),
            pl.BlockSpec((N_EXPERTS, tm), lambda i: (0, i)),
            pl.BlockSpec((N_EXPERTS, tm), lambda i: (0, i)),
            pl.BlockSpec((per_step, 8, N_EXPERTS), lambda i: (i, 0, 0)),
        ],
        out_shape=[
            jax.ShapeDtypeStruct((t, D_MODEL), F32),
            jax.ShapeDtypeStruct((t, D_MODEL), BF16),
            jax.ShapeDtypeStruct((N_EXPERTS, t), F32),
            jax.ShapeDtypeStruct((N_EXPERTS, t), F32),
            jax.ShapeDtypeStruct((nt * per_step, 8, N_EXPERTS), F32),
        ],
        compiler_params=_cparams(("arbitrary",)),
        name="outproj_router",
    )(or_p, or_s, oa_p, oa_s, x, mod_l.reshape(nc, 1, 6 * D_MODEL), w_out_l, ln_g_l, ln_b_l, wr_t, rbias)


def _sorted_layout(sel):
    n = sel.shape[1]
    selb = sel.astype(BF16)
    ti = lax.broadcasted_iota(jnp.int32, (n, n), 0)
    tj = lax.broadcasted_iota(jnp.int32, (n, n), 1)
    rank = _dot(selb, jnp.where(ti < tj, 1.0, 0.0).astype(BF16))
    cnt = _dot(selb, jnp.ones((n, n), BF16))
    nb = jnp.floor((cnt + (ROW_BLK - 1)) * (1.0 / ROW_BLK))
    ei = lax.broadcasted_iota(jnp.int32, (N_EXPERTS, N_EXPERTS), 0)
    ej = lax.broadcasted_iota(jnp.int32, (N_EXPERTS, N_EXPERTS), 1)
    boff = _dot(jnp.where(ej < ei, 1.0, 0.0).astype(BF16), nb.astype(BF16))
    eye = lax.broadcasted_iota(jnp.int32, (N_EXPERTS, n), 0) == lax.broadcasted_iota(jnp.int32, (N_EXPERTS, n), 1)
    ones8 = jnp.ones((8, N_EXPERTS), BF16)
    boff_row = _dot(ones8, jnp.where(eye, boff, 0.0).astype(BF16))[0:1, 0:N_EXPERTS]
    nb_row = _dot(ones8, jnp.where(eye, nb, 0.0).astype(BF16))[0:1, 0:N_EXPERTS]
    rankp = jnp.where(sel > 0.5, rank + 1.0, 0.0)
    return rankp, boff, boff_row, nb_row


def _row_expert_onehot(row0, n_rows, boff_row, nb_row):
    blk = (lax.broadcasted_iota(jnp.int32, (n_rows, N_EXPERTS), 0) + row0) // ROW_BLK
    blk = blk.astype(F32)
    return jnp.where(jnp.logical_and(blk >= boff_row, blk < boff_row + nb_row), 1.0, 0.0).astype(BF16)


def _sort_onehots(chunks, keys, boff_row, nb_row):
    n = keys.shape[1] // 2
    onehots = [_row_expert_onehot(c * DISP_ROWS, DISP_ROWS, boff_row, nb_row) for c in chunks]
    looked = [_dot(e, keys) for e in onehots]
    out = []
    for c, qk in zip(chunks, looked):
        row = lax.broadcasted_iota(jnp.int32, (DISP_ROWS, n), 0) + (c * DISP_ROWS + 1)
        out.append(jnp.where(qk[:, 0:n] == row.astype(F32) - ROW_BLK * qk[:, n:2 * n], 1.0, 0.0).astype(BF16))
    return out


def _dispatch_kernel(used_ref, h2_ref, sel_ref, wt_ref, xs_ref, xa_scr):
    s = pl.program_id(0)
    n = h2_ref.shape[0]
    rankp, boff, boff_row, nb_row = _sorted_layout(sel_ref[...])
    keys = jnp.concatenate([rankp, boff], axis=1).astype(BF16)
    ti = lax.broadcasted_iota(jnp.int32, (n, n), 0)
    tj = lax.broadcasted_iota(jnp.int32, (n, n), 1)
    eye = jnp.where(ti == tj, 1.0, 0.0).astype(BF16)
    w1, w2, w3 = _split3(wt_ref[...])
    xa_scr[:, 0:D_MODEL] = h2_ref[...]
    xa_scr[:, D_MODEL:D_MODEL + 128] = _dot_nt(eye, jnp.concatenate([w1, w2], axis=0)).astype(BF16)
    xa_scr[:, D_MODEL + 128:XS_COLS] = _dot_nt(eye, jnp.concatenate([w3, jnp.zeros_like(w3)], axis=0)).astype(BF16)

    def emit(chunks):
        for c, g in zip(chunks, _sort_onehots(chunks, keys, boff_row, nb_row)):
            xs_ref[c * DISP_ROWS:(c + 1) * DISP_ROWS, :] = _dot(g, xa_scr[...]).astype(BF16)

    always = (MOE_TILE * TOP_K) // DISP_ROWS
    emit(tuple(range(always)))
    for c in range(always, TILE_ROWS // DISP_ROWS):
        @pl.when(c * DISP_ROWS < used_ref[s])
        def _():
            emit((c,))

        @pl.when(c * DISP_ROWS >= used_ref[s])
        def _():
            xs_ref[c * DISP_ROWS:(c + 1) * DISP_ROWS, :] = jnp.zeros((DISP_ROWS, XS_COLS), BF16)


def _dispatch(used_rows, h2, sel_t, w_t):
    t = h2.shape[0]
    nt = t // MOE_TILE
    return pl.pallas_call(
        _dispatch_kernel,
        grid_spec=pltpu.PrefetchScalarGridSpec(
            num_scalar_prefetch=1,
            grid=(nt,),
            in_specs=[
                pl.BlockSpec((MOE_TILE, D_MODEL), lambda s, u: (s, 0)),
                pl.BlockSpec((N_EXPERTS, MOE_TILE), lambda s, u: (0, s)),
                pl.BlockSpec((N_EXPERTS, MOE_TILE), lambda s, u: (0, s)),
            ],
            out_specs=pl.BlockSpec((TILE_ROWS, XS_COLS), lambda s, u: (s, 0)),
            scratch_shapes=[pltpu.VMEM((MOE_TILE, XS_COLS), BF16)],
        ),
        out_shape=jax.ShapeDtypeStruct((nt * TILE_ROWS, XS_COLS), BF16),
        compiler_params=_cparams(("arbitrary",)),
        name="moe_dispatch",
    )(used_rows, h2, sel_t, w_t)


def _expert_kernel(bstart_ref, bsrc_ref, xs_hbm, w1_ref, w2_ref, ys_hbm, w1_scr, w2_scr, x_buf, y_buf, st_ref,
                   sem_in, sem_out):
    e = pl.program_id(0)
    b0 = bstart_ref[e]
    n_blk = bstart_ref[e + 1] - b0
    n_groups = (n_blk + EXP_GROUP - 1) // EXP_GROUP

    def in_copy(j, slot, k):
        src = pl.multiple_of(bsrc_ref[j] * ROW_BLK, ROW_BLK)
        row = k * ROW_BLK if isinstance(k, int) else pl.multiple_of(k * ROW_BLK, ROW_BLK)
        return pltpu.make_async_copy(xs_hbm.at[pl.ds(src, ROW_BLK), :],
                                     x_buf.at[slot, pl.ds(row, ROW_BLK), :], sem_in.at[slot])

    def out_copy(j, slot, k):
        dst = pl.multiple_of(bsrc_ref[j] * ROW_BLK, ROW_BLK)
        row = k * ROW_BLK if isinstance(k, int) else pl.multiple_of(k * ROW_BLK, ROW_BLK)
        return pltpu.make_async_copy(y_buf.at[slot, pl.ds(row, ROW_BLK), :],
                                     ys_hbm.at[pl.ds(dst, ROW_BLK), pl.ds(0, D_MODEL)], sem_out.at[slot])

    def for_blocks(pos0, cnt, fn):
        @pl.when(cnt == EXP_GROUP)
        def _():
            for k in range(EXP_GROUP):
                fn(pos0 + k, k)

        @pl.when(cnt < EXP_GROUP)
        def _():
            def body(k, carry):
                fn(pos0 + k, k)
                return carry

            lax.fori_loop(0, cnt, body, 0)

    @pl.when(e == 0)
    def _():
        x_buf[...] = jnp.zeros(x_buf.shape, BF16)
        for i in range(6):
            st_ref[i] = 0

    g0 = st_ref[0]
    issued = st_ref[5]
    st_ref[5] = 0

    @pl.when(jnp.logical_and(n_groups > 0, issued == 0))
    def _():
        for_blocks(b0, jnp.minimum(n_blk, EXP_GROUP), lambda j, k: in_copy(j, g0 % 2, k).start())

    w1_scr[...] = w1_ref[...].astype(BF16)
    w2_scr[...] = w2_ref[...].astype(BF16)

    nxt = jnp.minimum(e + 1, N_EXPERTS)
    next_b0 = bstart_ref[nxt]
    next_n = bstart_ref[jnp.minimum(e + 2, N_EXPERTS)] - next_b0

    def group(gi, carry):
        slot = (g0 + gi) % 2
        pos = b0 + gi * EXP_GROUP
        cnt = jnp.minimum(n_blk - gi * EXP_GROUP, EXP_GROUP)
        last = gi + 1 >= n_groups

        @pl.when(jnp.logical_not(last))
        def _():
            for_blocks(pos + EXP_GROUP, jnp.minimum(n_blk - (gi + 1) * EXP_GROUP, EXP_GROUP),
                       lambda j, k: in_copy(j, 1 - slot, k).start())

        @pl.when(jnp.logical_and(last, next_n > 0))
        def _():
            for_blocks(next_b0, jnp.minimum(next_n, EXP_GROUP), lambda j, k: in_copy(j, 1 - slot, k).start())
            st_ref[5] = 1

        for_blocks(pos, cnt, lambda j, k: in_copy(j, slot, k).wait())

        old_pos = st_ref[1 + 2 * slot]
        old_cnt = st_ref[2 + 2 * slot]

        @pl.when(old_cnt > 0)
        def _():
            for_blocks(old_pos, old_cnt, lambda j, k: out_copy(j, slot, k).wait())

        def ffn(rows):
            x = x_buf[slot, 0:rows, 0:D_MODEL]
            lane = lax.broadcasted_iota(jnp.int32, (rows, W_LANES), 1)
            mine = (lane & (N_EXPERTS - 1)) == e
            wrow = jnp.sum(jnp.where(mine, x_buf[slot, 0:rows, D_MODEL:XS_COLS].astype(F32), 0.0),
                           axis=1, keepdims=True)
            h = _dot(x, w1_scr[...])
            act = (_silu(h[:, 0:D_EXPERT]) * h[:, D_EXPERT:2 * D_EXPERT]).astype(BF16)
            y_buf[slot, 0:rows, :] = (_dot(act, w2_scr[...]) * wrow).astype(BF16)

        full_rows = EXP_GROUP * ROW_BLK
        short = cnt <= EXP_GROUP // 2

        @pl.when(jnp.logical_not(short))
        def _():
            ffn(full_rows)

        @pl.when(short)
        def _():
            ffn(full_rows // 2)

        for_blocks(pos, cnt, lambda j, k: out_copy(j, slot, k).start())
        st_ref[1 + 2 * slot] = pos
        st_ref[2 + 2 * slot] = cnt
        return carry

    lax.fori_loop(0, n_groups, group, 0)
    st_ref[0] = g0 + n_groups

    @pl.when(e == N_EXPERTS - 1)
    def _():
        for slot in range(2):
            @pl.when(st_ref[2 + 2 * slot] > 0)
            def _():
                for_blocks(st_ref[1 + 2 * slot], st_ref[2 + 2 * slot], lambda j, k: out_copy(j, slot, k).wait())


def _experts(bstart, bsrc, xs, w_e_in, w_e_out, layer):
    rows = EXP_GROUP * ROW_BLK
    return pl.pallas_call(
        _expert_kernel,
        grid_spec=pltpu.PrefetchScalarGridSpec(
            num_scalar_prefetch=2,
            grid=(N_EXPERTS,),
            in_specs=[
                pl.BlockSpec(memory_space=pl.ANY),
                pl.BlockSpec((None, None, D_MODEL, 2 * D_EXPERT), lambda e, bs, br: (layer, e, 0, 0)),
                pl.BlockSpec((None, None, D_EXPERT, D_MODEL), lambda e, bs, br: (layer, e, 0, 0)),
            ],
            out_specs=pl.BlockSpec(memory_space=pl.ANY),
            scratch_shapes=[
                pltpu.VMEM((D_MODEL, 2 * D_EXPERT), BF16),
                pltpu.VMEM((D_EXPERT, D_MODEL), BF16),
                pltpu.VMEM((2, rows, XS_COLS), BF16),
                pltpu.VMEM((2, rows, D_MODEL), BF16),
                pltpu.SMEM((8,), jnp.int32),
                pltpu.SemaphoreType.DMA((2,)),
                pltpu.SemaphoreType.DMA((2,)),
            ],
        ),
        out_shape=jax.ShapeDtypeStruct(xs.shape, xs.dtype),
        input_output_aliases={2: 0},
        compiler_params=_cparams(("arbitrary",)),
        name="moe_experts",
    )(bstart, bsrc, xs, w_e_in, w_e_out)


def _combine_kernel(used_ref, ys_ref, sel_ref, h2_ref, x1_ref, mod_ref, wsi_ref, wso_ref, lng_ref, lnb_ref,
                    o_ref, acc_scr, keyt_scr, row_scr, *, alpha, tile0):
    s = pl.program_id(0) + tile0
    half = pl.program_id(1)
    n = h2_ref.shape[0]
    per_step = TILE_ROWS // DISP_ROWS // COMB_SPLIT
    always = (MOE_TILE * TOP_K) // DISP_ROWS

    @pl.when(half == 0)
    def _():
        rankp, _, boff_row, nb_row = _sorted_layout(sel_ref[...])
        ti = lax.broadcasted_iota(jnp.int32, (n, n), 0)
        tj = lax.broadcasted_iota(jnp.int32, (n, n), 1)
        eye = jnp.where(ti == tj, 1.0, 0.0).astype(BF16)
        keyt_scr[0:n, :] = _dot_nt(eye, rankp.astype(BF16)).astype(BF16)
        keyt_scr[n:2 * n, :] = jnp.broadcast_to(boff_row, (n, N_EXPERTS)).astype(BF16)
        row_scr[0:1, :] = boff_row
        row_scr[1:2, :] = nb_row
        h = _dot(h2_ref[...], wsi_ref[...])
        act = (_silu(h[:, 0:D_SHARED]) * h[:, D_SHARED:2 * D_SHARED]).astype(BF16)
        acc_scr[...] = _dot(act, wso_ref[...])

    def unsort(chunks):
        onehots = [_row_expert_onehot(g * DISP_ROWS, DISP_ROWS, row_scr[0:1, :], row_scr[1:2, :]) for g in chunks]
        looked = [_dot_nt(keyt_scr[...], e) for e in onehots]
        total = None
        for g, qk in zip(chunks, looked):
            row = lax.broadcasted_iota(jnp.int32, (n, DISP_ROWS), 1) + (g * DISP_ROWS + 1)
            c = jnp.where(qk[0:n] == row.astype(F32) - ROW_BLK * qk[n:2 * n], 1.0, 0.0).astype(BF16)
            lo = (g % per_step) * DISP_ROWS
            part = _dot(c, ys_ref[lo:lo + DISP_ROWS, :])
            total = part if total is None else total + part
        acc_scr[...] += total

    for step in range(COMB_SPLIT):
        @pl.when(half == step)
        def _():
            mine = range(step * per_step, (step + 1) * per_step)
            sure = tuple(g for g in mine if g < always)
            if sure:
                unsort(sure)
            for g in mine:
                if g >= always:
                    @pl.when(g * DISP_ROWS < used_ref[s])
                    def _():
                        unsort((g,))

    @pl.when(half == COMB_SPLIT - 1)
    def _():
        g2 = mod_ref[:, 5 * D_MODEL:6 * D_MODEL]
        y = _layer_norm(alpha * x1_ref[...] + g2 * acc_scr[...])
        o_ref[...] = y * lng_ref[1:2, :] + lnb_ref[1:2, :]


def _combine(used_rows, ys, sel_t, h2, x1, mod_l, w_sh_in_l, w_sh_out_l, ln_g_l, ln_b_l, alpha, cond_of_tile,
             tile0=0, n_tiles=None):
    nt = h2.shape[0] // MOE_TILE - tile0 if n_tiles is None else n_tiles
    nrc = COMB_SPLIT
    comb_rows = TILE_ROWS // COMB_SPLIT
    nc = mod_l.shape[0]
    kern = functools.partial(_combine_kernel, alpha=alpha, tile0=tile0)
    row = lambda s, r, u: (s + tile0, 0)
    const = lambda s, r, u: (0, 0)
    return pl.pallas_call(
        kern,
        grid_spec=pltpu.PrefetchScalarGridSpec(
            num_scalar_prefetch=1,
            grid=(nt, nrc),
            in_specs=[
                pl.BlockSpec((comb_rows, D_MODEL), lambda s, r, u: ((s + tile0) * nrc + r, 0)),
                pl.BlockSpec((N_EXPERTS, MOE_TILE), lambda s, r, u: (0, s + tile0)),
                pl.BlockSpec((MOE_TILE, D_MODEL), row),
                pl.BlockSpec((MOE_TILE, D_MODEL), row),
                pl.BlockSpec((None, 1, 6 * D_MODEL), lambda s, r, u: (cond_of_tile(s + tile0), 0, 0)),
                pl.BlockSpec((D_MODEL, 2 * D_SHARED), const),
                pl.BlockSpec((D_SHARED, D_MODEL), const),
                pl.BlockSpec((2, D_MODEL), const),
                pl.BlockSpec((2, D_MODEL), const),
            ],
            out_specs=pl.BlockSpec((MOE_TILE, D_MODEL), lambda s, r, u: (s, 0)),
            scratch_shapes=[
                pltpu.VMEM((MOE_TILE, D_MODEL), F32),
                pltpu.VMEM((2 * MOE_TILE, N_EXPERTS), BF16),
                pltpu.VMEM((8, N_EXPERTS), F32),
            ],
        ),
        out_shape=jax.ShapeDtypeStruct((nt * MOE_TILE, D_MODEL), F32),
        compiler_params=_cparams(("arbitrary", "arbitrary")),
        name="moe_combine",
    )(used_rows, ys, sel_t, h2, x1, mod_l.reshape(nc, 1, 6 * D_MODEL), w_sh_in_l, w_sh_out_l, ln_g_l, ln_b_l)


def _block_lists(nb):
    nt = nb.shape[0]
    cum = jnp.cumsum(nb, axis=1)
    boff = cum - nb
    used_blk = cum[:, -1]
    per_e = jnp.sum(nb, axis=0)
    bstart = jnp.concatenate([jnp.zeros((1,), jnp.int32), jnp.cumsum(per_e).astype(jnp.int32)])
    pref = jnp.cumsum(nb, axis=0) - nb
    lb = jnp.arange(TILE_BLKS, dtype=jnp.int32)
    owner = (lb[None, :, None] >= boff[:, None, :]) & (lb[None, :, None] < cum[:, None, :])
    base = bstart[None, :-1] + pref - boff
    pos = jnp.sum(jnp.where(owner, base[:, None, :], 0), axis=-1) + lb[None, :]
    total = nt * TILE_BLKS
    pos = jnp.where(jnp.any(owner, axis=-1), pos, total)
    src = (jnp.arange(nt, dtype=jnp.int32)[:, None] * TILE_BLKS + lb[None, :]).reshape(-1)
    bsrc = jnp.zeros((total,), jnp.int32).at[pos.reshape(-1)].set(src, mode="drop")
    return (used_blk * ROW_BLK).astype(jnp.int32), bstart, bsrc


def _lower_bound_consts(lb_raw):
    p = jax.nn.softmax(lb_raw.astype(F32), axis=0)
    cs = jnp.cumsum(p, axis=0)
    lb = jnp.clip(jnp.concatenate([jnp.zeros_like(cs[:1]), cs[:-1]], axis=0), 0.0, LB_MAX)
    rows = jnp.stack([jnp.log(lb[:, 0]), jnp.log1p(-lb[:, 0]), 1.0 - lb[:, 0],
                      jnp.log(lb[:, 1]), jnp.log1p(-lb[:, 1]), 1.0 - lb[:, 1]], axis=1)
    return jnp.concatenate([rows, jnp.zeros((rows.shape[0], 2, W_R), F32)], axis=1)


def _rope_tables(seq_len):
    t = jnp.arange(seq_len)
    half = HEAD_DIM // 2
    nf = half // 2
    inv = ROPE_THETA ** (-jnp.arange(nf, dtype=F32) / nf)

    def tabs(pos):
        ang = pos[:, None].astype(F32) * inv[None, :]
        cos = jnp.concatenate([jnp.cos(ang), jnp.cos(ang)], -1)
        sin = jnp.sin(ang)
        zero = jnp.zeros_like(sin)
        return cos, jnp.concatenate([-sin, zero], -1), jnp.concatenate([zero, sin], -1)

    row = tabs(t // GRID_W)
    colt = tabs(t % GRID_W)
    return tuple(jnp.concatenate([a, b], -1) for a, b in zip(row, colt))


def _bias_tables(rpb):
    cq = jnp.arange(GRID_W)
    cs = jnp.clip(cq - WIN_C // 2, 0, GRID_W - WIN_C)
    valid = (cq[None, :] >= cs[:, None]) & (cq[None, :] < cs[:, None] + WIN_C)
    coff = jnp.clip(cq[None, :] - cq[:, None], -(WIN_C - 1), WIN_C - 1) + WIN_C - 1
    roff = jnp.arange(WIN_R)[:, None] + jnp.arange(WIN_R)[None, :]
    pick_c = (coff[None] == jnp.arange(2 * WIN_C - 1)[:, None, None]).astype(F32)
    pick_r = (roff[None] == jnp.arange(2 * WIN_R - 1)[:, None, None]).astype(F32)
    hi = lax.Precision.HIGHEST
    bias_c = jnp.einsum('dhrc,cqw->dhrqw', rpb.astype(F32), pick_c, precision=hi)
    tab = jnp.einsum('rlk,dhrqw->dhlqkw', pick_r, bias_c, precision=hi)
    tab = jnp.where(valid[None, None, None, :, None, :], tab, NEG_BIG)
    return tab.reshape(tab.shape[0], H_A, WIN_R, GRID_W, WIN_R * GRID_W)


def kernel(x_prompt, x_sample, cache_k, cache_v, state_hgrn, c, c_ctx, w_ada, b_ada, w_in, w_out, lb_raw, hgrn_norm,
           rpb, ln_g, ln_b, w_router, router_bias, w_e_in, w_e_out, w_sh_in, w_sh_out):
    n_p, seq, d = x_prompt.shape
    n_s, dseq, _ = x_sample.shape
    depth = w_in.shape[0]
    t_p, t_s = n_p * seq, n_s * dseq
    t = t_p + t_s
    alpha = (2 * depth) ** 0.25
    assert d == D_MODEL and seq % MOE_TILE == 0 and dseq % (2 * MOE_TILE) == 0 and t_p % dseq == 0
    assert dseq % GRID_W == 0 and dseq // GRID_W >= WIN_R

    tm_in = next(m for m in (1024, 512, 256) if t_p % m == 0 and dseq % m == 0)

    def cond_of_tile(tile_rows):
        n_p_tiles = t_p // tile_rows
        per_seq = dseq // tile_rows
        return lambda i: jnp.where(i < n_p_tiles, 0, 1 + (i - n_p_tiles) // per_seq)

    n_cond = -(-(1 + n_s) // 8) * 8
    cond = jnp.zeros((n_cond, d), F32).at[0].set(c_ctx).at[1:1 + n_s].set(c)
    mod = _modulation(cond, w_ada, b_ada)

    lbc = _lower_bound_consts(lb_raw)
    rope_tabs = _rope_tables(dseq)
    bias_tabs = _bias_tables(rpb)
    w_in_b = w_in.astype(BF16)
    w_out_b = w_out.astype(BF16)
    w_sh_in_b = w_sh_in.astype(BF16)
    w_sh_out_b = w_sh_out.astype(BF16)
    wr_t = jnp.swapaxes(w_router, 1, 2).astype(BF16)

    x = jnp.concatenate([x_prompt.reshape(t_p, d), x_sample.reshape(t_s, d)], axis=0)
    new_k, new_v, new_s = [], [], []
    for l in range(depth):
        proj = _inproj(x, mod[l], w_in_b[l], tm_in, cond_of_tile(tm_in))
        nrm = hgrn_norm[l].reshape(1, HEAD_DIM)
        or_p, st_p = _hgrn(proj, lbc[l], nrm, seq, n_p, 0, emit_state=True)
        (or_s,) = _hgrn(proj, lbc[l], nrm, dseq, n_s, t_p // dseq, state_in=state_hgrn, layer=l)
        oa_p, k_l, v_l = _ctx_attention(proj, seq, n_p)
        oa_s = _nbr_attention(proj, cache_k, cache_v, bias_tabs[l], rope_tabs, l, dseq, n_s, t_p // dseq)
        x1, h2, w_t, sel_t, nb = _outproj(or_p, or_s, oa_p, oa_s, x, mod[l], w_out_b[l], ln_g[l], ln_b[l], wr_t[l],
                                          router_bias[l].reshape(N_EXPERTS, 1), alpha, cond_of_tile(2 * MOE_TILE))
        used_rows, bstart, bsrc = _block_lists(nb[:, 0, :].astype(jnp.int32))
        xs = _dispatch(used_rows, h2, sel_t, w_t)
        ys = _experts(bstart, bsrc, xs, w_e_in, w_e_out, l)
        comb = functools.partial(_combine, used_rows, ys, sel_t, h2, x1, mod[l], w_sh_in_b[l], w_sh_out_b[l],
                                 ln_g[l], ln_b[l], alpha, cond_of_tile(MOE_TILE))
        if l + 1 < depth:
            x = comb()
        else:
            y_p = comb(tile0=0, n_tiles=t_p // MOE_TILE)
            y_s = comb(tile0=t_p // MOE_TILE, n_tiles=t_s // MOE_TILE)
        new_k.append(k_l)
        new_v.append(v_l)
        new_s.append(st_p)
    return (y_p.reshape(n_p, seq, d), y_s.reshape(n_s, dseq, d),
            jnp.stack(new_k, axis=1), jnp.stack(new_v, axis=1), jnp.stack(new_s, axis=1))
```
